```python
import jax, jax.numpy as jnp
from jax import lax
import numpy as np

D_MODEL = 1024
BATCH = 8
SEQ = 8192
DEPTH = 1

HEAD_DIM = 64
DIL_GROUPS = ((128, 1), (512, 4), (2048, 16))
N_DIL = len(DIL_GROUPS)
DIL_HEADS = 8
SWA_WINDOW = 128
SWA_Q_HEADS = 8
SWA_KV_HEADS = 2
SWA_GROUP = SWA_Q_HEADS // SWA_KV_HEADS
BLOCK = 128
ROPE_THETA = 10000.0
MEM_LEN = 256
X_HEADS = 4
X_HEAD_DIM = D_MODEL // X_HEADS
D_FF = 4 * D_MODEL
EPS = 1e-6
N_BRANCH = 2

DIL_WIDTH = N_DIL * DIL_HEADS * HEAD_DIM
DIL_OUT = DIL_HEADS * HEAD_DIM
SWA_Q_WIDTH = SWA_Q_HEADS * HEAD_DIM
SWA_KV_WIDTH = SWA_KV_HEADS * HEAD_DIM
GATE_WIDTH = N_BRANCH * D_MODEL
IN_WIDTH = 3 * DIL_WIDTH + SWA_Q_WIDTH + 2 * SWA_KV_WIDTH + GATE_WIDTH

kernel_name = "hybrid_dilated_swa_sink_gated_block"


def rmsnorm(x, g):
    xf = x.astype(jnp.float32)
    y = xf * lax.rsqrt(jnp.mean(xf * xf, axis=-1, keepdims=True) + EPS)
    return (y * g.astype(jnp.float32)).astype(x.dtype)


def rope(x, pos):
    half = x.shape[-1] // 2
    inv = ROPE_THETA ** (-jnp.arange(half, dtype=jnp.float32) / half)
    ang = pos.astype(jnp.float32)[..., None] * inv
    cos = jnp.cos(ang)[:, :, None, :]
    sin = jnp.sin(ang)[:, :, None, :]
    xf = x.astype(jnp.float32)
    x1, x2 = xf[..., :half], xf[..., half:]
    out = jnp.concatenate([x1 * cos - x2 * sin, x2 * cos + x1 * sin], axis=-1)
    return out.astype(x.dtype)


def banded_attention(q, k, v, window, sink=None):
    assert window <= BLOCK
    N, L, Hkv, G, D = q.shape
    nb = -(-L // BLOCK)
    Lp = nb * BLOCK
    pad = Lp - L
    if pad:
        q = jnp.pad(q, ((0, 0), (0, pad), (0, 0), (0, 0), (0, 0)))
        k = jnp.pad(k, ((0, 0), (0, pad), (0, 0), (0, 0)))
        v = jnp.pad(v, ((0, 0), (0, pad), (0, 0), (0, 0)))
    qb = q.reshape(N, nb, BLOCK, Hkv, G, D)
    kb = k.reshape(N, nb, BLOCK, Hkv, D)
    vb = v.reshape(N, nb, BLOCK, Hkv, D)
    shift = ((0, 0), (1, 0), (0, 0), (0, 0), (0, 0))
    kcat = jnp.concatenate([jnp.pad(kb, shift)[:, :-1], kb], axis=2)
    vcat = jnp.concatenate([jnp.pad(vb, shift)[:, :-1], vb], axis=2)
    s = jnp.einsum('nbqhgd,nbkhd->nbhgqk', qb, kcat,
                   preferred_element_type=jnp.float32) * (D ** -0.5)
    qi = jnp.arange(BLOCK)[:, None] + BLOCK
    ki = jnp.arange(2 * BLOCK)[None, :]
    dist = qi - ki
    key_glob = jnp.arange(nb)[:, None, None] * BLOCK - BLOCK + ki[None]
    mask = ((dist >= 0) & (dist <= window))[None] & (key_glob >= 0)
    s = jnp.where(mask[None, :, None, None], s, -jnp.inf)
    m = jnp.max(s, axis=-1, keepdims=True)
    if sink is not None:
        sink_b = sink.astype(jnp.float32)[None, None, :, :, None, None]
        m = jnp.maximum(m, sink_b)
    p = jnp.exp(s - m)
    den = jnp.sum(p, axis=-1, keepdims=True)
    if sink is not None:
        den = den + jnp.exp(sink_b - m)
    lse = (m + jnp.log(den))[..., 0]
    o = jnp.einsum('nbhgqk,nbkhd->nbqhgd', p / den, vcat.astype(jnp.float32))
    o = o.reshape(N, Lp, Hkv, G, D)[:, :L].astype(v.dtype)
    lse = lse.transpose(0, 1, 4, 2, 3).reshape(N, Lp, Hkv, G)[:, :L]
    return o, lse


def dilated_attention(q, k, v):
    B, S, _, H, D = q.shape
    outs, lses = [], []
    for gi, (w, d) in enumerate(DIL_GROUPS):
        Ls = S // d
        def to_res(a):
            return a.reshape(B, Ls, d, H, D).swapaxes(1, 2).reshape(B * d, Ls, H, D)
        o, lse = banded_attention(to_res(q[:, :, gi])[:, :, :, None, :],
                                  to_res(k[:, :, gi]), to_res(v[:, :, gi]), w // d)
        o = o[:, :, :, 0].reshape(B, d, Ls, H, D).swapaxes(1, 2).reshape(B, S, H, D)
        lse = lse[..., 0].reshape(B, d, Ls, H).swapaxes(1, 2).reshape(B, S, H)
        outs.append(o)
        lses.append(lse)
    O = jnp.stack(outs, axis=2).astype(jnp.float32)
    alpha = jax.nn.softmax(jnp.stack(lses, axis=2), axis=2)
    return jnp.sum(alpha[..., None] * O, axis=2).astype(q.dtype)


def mixer_sublayer(x, positions, g_mix, w_in, b_gate, sink, w_branch_a, w_branch_b, w_out):
    B, S, _ = x.shape
    h = rmsnorm(x, g_mix)
    proj = jnp.einsum('bsd,de->bse', h, w_in)
    c = np.cumsum([DIL_WIDTH, DIL_WIDTH, DIL_WIDTH, SWA_Q_WIDTH, SWA_KV_WIDTH, SWA_KV_WIDTH])
    qa, ka, va, qb, kb, vb, gpre = jnp.split(proj, [int(i) for i in c], axis=-1)
    na = N_DIL * DIL_HEADS
    qa = rope(qa.reshape(B, S, na, HEAD_DIM), positions).reshape(B, S, N_DIL, DIL_HEADS, HEAD_DIM)
    ka = rope(ka.reshape(B, S, na, HEAD_DIM), positions).reshape(B, S, N_DIL, DIL_HEADS, HEAD_DIM)
    va = va.reshape(B, S, N_DIL, DIL_HEADS, HEAD_DIM)
    oa = dilated_attention(qa, ka, va).reshape(B, S, DIL_OUT)
    qb = rope(qb.reshape(B, S, SWA_Q_HEADS, HEAD_DIM), positions)
    kb = rope(kb.reshape(B, S, SWA_KV_HEADS, HEAD_DIM), positions)
    vb = vb.reshape(B, S, SWA_KV_HEADS, HEAD_DIM)
    ob, _ = banded_attention(qb.reshape(B, S, SWA_KV_HEADS, SWA_GROUP, HEAD_DIM), kb, vb,
                             SWA_WINDOW, sink=sink.reshape(SWA_KV_HEADS, SWA_GROUP))
    ob = ob.reshape(B, S, SWA_Q_WIDTH)
    gates = jax.nn.sigmoid((gpre + b_gate.reshape(GATE_WIDTH)).astype(jnp.float32))
    gates = gates.reshape(B, S, N_BRANCH, D_MODEL).astype(x.dtype)
    ya = jnp.einsum('bse,ed->bsd', oa, w_branch_a)
    yb = jnp.einsum('bse,ed->bsd', ob, w_branch_b)
    merged = gates[:, :, 0] * ya + gates[:, :, 1] * yb
    return x + jnp.einsum('bsd,de->bse', merged, w_out)


def cross_sublayer(x, mem, g_cross, g_mem, w_cq, w_ckv, w_co):
    B, S, _ = x.shape
    M = mem.shape[1]
    hc = rmsnorm(x, g_cross)
    mn = rmsnorm(mem, g_mem)
    q = jnp.einsum('bsd,de->bse', hc, w_cq).reshape(B, S, X_HEADS, X_HEAD_DIM)
    k, v = jnp.split(jnp.einsum('bmd,de->bme', mn, w_ckv), 2, axis=-1)
    k = k.reshape(B, M, X_HEADS, X_HEAD_DIM)
    v = v.reshape(B, M, X_HEADS, X_HEAD_DIM)
    s = jnp.einsum('bshd,bmhd->bhsm', q, k, preferred_element_type=jnp.float32) * (X_HEAD_DIM ** -0.5)
    p = jax.nn.softmax(s, axis=-1)
    o = jnp.einsum('bhsm,bmhd->bshd', p, v.astype(jnp.float32)).astype(x.dtype).reshape(B, S, D_MODEL)
    return x + jnp.einsum('bse,ed->bsd', o, w_co)


def mlp_sublayer(x, g_mlp, w_1, w_2):
    h = rmsnorm(x, g_mlp)
    a = jax.nn.relu(jnp.einsum('bsd,df->bsf', h, w_1))
    return x + jnp.einsum('bsf,fd->bsd', a * a, w_2)


def setup_inputs(seed: int = 0) -> dict:
    key = jax.random.key(seed)
    ks = jax.random.split(key, 20)
    f32 = jnp.float32
    def nrm(k, shape, fan_in):
        return jax.random.normal(k, shape, f32) * (fan_in ** -0.5)
    def gain(k, shape):
        return 1.0 + 0.01 * jax.random.normal(k, shape, f32)
    x = jax.random.normal(ks[0], (BATCH, SEQ, D_MODEL), f32)
    mem = jax.random.normal(ks[1], (BATCH, MEM_LEN, D_MODEL), f32)
    start = jax.random.randint(ks[2], (BATCH,), 0, 4096, dtype=jnp.int32)
    positions = start[:, None] + jnp.arange(SEQ, dtype=jnp.int32)[None, :]
    return {
        "x": x,
        "mem": mem,
        "positions": positions,
        "g_mix": gain(ks[3], (DEPTH, D_MODEL)),
        "w_in": nrm(ks[4], (DEPTH, D_MODEL, IN_WIDTH), D_MODEL),
        "b_gate": 0.01 * jax.random.normal(ks[5], (DEPTH, N_BRANCH, D_MODEL), f32),
        "sink": 0.5 * jax.random.normal(ks[6], (DEPTH, SWA_Q_HEADS), f32),
        "w_branch_a": nrm(ks[7], (DEPTH, DIL_OUT, D_MODEL), DIL_OUT),
        "w_branch_b": nrm(ks[8], (DEPTH, SWA_Q_WIDTH, D_MODEL), SWA_Q_WIDTH),
        "w_out": nrm(ks[9], (DEPTH, D_MODEL, D_MODEL), D_MODEL),
        "g_cross": gain(ks[10], (DEPTH, D_MODEL)),
        "g_mem": gain(ks[11], (DEPTH, D_MODEL)),
        "w_cq": nrm(ks[12], (DEPTH, D_MODEL, D_MODEL), D_MODEL),
        "w_ckv": nrm(ks[13], (DEPTH, D_MODEL, 2 * D_MODEL), D_MODEL),
        "w_co": nrm(ks[14], (DEPTH, D_MODEL, D_MODEL), D_MODEL),
        "g_mlp": gain(ks[15], (DEPTH, D_MODEL)),
        "w_1": nrm(ks[16], (DEPTH, D_MODEL, D_FF), D_MODEL),
        "w_2": nrm(ks[17], (DEPTH, D_FF, D_MODEL), D_FF),
        "g_final": gain(ks[18], (D_MODEL,)),
    }


def reference(x, mem, positions, g_mix, w_in, b_gate, sink, w_branch_a, w_branch_b, w_out,
              g_cross, g_mem, w_cq, w_ckv, w_co, g_mlp, w_1, w_2, g_final):
    for l in range(DEPTH):
        x = mixer_sublayer(x, positions, g_mix[l], w_in[l], b_gate[l], sink[l],
                           w_branch_a[l], w_branch_b[l], w_out[l])
        x = cross_sublayer(x, mem, g_cross[l], g_mem[l], w_cq[l], w_ckv[l], w_co[l])
        x = mlp_sublayer(x, g_mlp[l], w_1[l], w_2[l])
    return rmsnorm(x, g_final)
```

```python
import functools

import jax
import jax.numpy as jnp
from jax import lax
from jax.experimental import pallas as pl
from jax.experimental.pallas import tpu as pltpu

D_MODEL = 1024
HEAD_DIM = 64
HALF = HEAD_DIM // 2
DIL_GROUPS = ((128, 1), (512, 4), (2048, 16))
N_DIL = len(DIL_GROUPS)
DIL_HEADS = 8
SWA_Q_HEADS = 8
SWA_KV_HEADS = 2
BLOCK = 128
ROPE_THETA = 10000.0
X_HEADS = 4
X_HEAD_DIM = D_MODEL // X_HEADS
D_FF = 4 * D_MODEL
EPS = 1e-6
N_BRANCH = 2

DIL_WIDTH = N_DIL * DIL_HEADS * HEAD_DIM
DIL_OUT = DIL_HEADS * HEAD_DIM
SWA_Q_WIDTH = SWA_Q_HEADS * HEAD_DIM
SWA_KV_WIDTH = SWA_KV_HEADS * HEAD_DIM
GATE_WIDTH = N_BRANCH * D_MODEL

LANES = 128
HG = 4
HGW = HG * HEAD_DIM
GROUP_W = 2 * HGW

COL_DIL = 0
COL_SWA_Q = N_DIL * 3 * GROUP_W
COL_SWA_K = COL_SWA_Q + GROUP_W
COL_SWA_V = COL_SWA_K + GROUP_W
COL_GATE = COL_SWA_V + GROUP_W
PROJ_W = COL_GATE + GATE_WIDTH

TM = 512
QT = 512
NEG = -1e30
VMEM_LIMIT = 56 * 1024 * 1024


def _const_spec(shape):
    nd = len(shape)
    return pl.BlockSpec(shape, lambda *_: (0,) * nd, pipeline_mode=pl.Buffered(1))


def _params(n_axes):
    return pltpu.CompilerParams(dimension_semantics=("parallel",) * n_axes,
                                vmem_limit_bytes=VMEM_LIMIT)


def _rms(x, g):
    ms = jnp.mean(x * x, axis=-1, keepdims=True)
    return x * lax.rsqrt(ms + EPS) * g


def _proj_kernel(x_ref, g_ref, w_ref, bg_ref, cos_ref, sin_ref, o_ref):
    h = _rms(x_ref[...], g_ref[...]).astype(jnp.bfloat16)
    cos = cos_ref[...]
    sin = sin_ref[...]
    for c in range(0, PROJ_W, HGW):
        r = jnp.dot(h, w_ref[:, c:c + HGW], preferred_element_type=jnp.float32)
        if c >= COL_GATE:
            z = r + bg_ref[:, c - COL_GATE:c - COL_GATE + HGW]
            out = 1.0 / (1.0 + jnp.exp(-z))
        elif c >= COL_SWA_V or (c < COL_SWA_Q and (c % (3 * GROUP_W)) >= 2 * GROUP_W):
            out = r
        else:
            a = r[:, :LANES]
            b = r[:, LANES:]
            out = jnp.concatenate([a * cos - b * sin, b * cos + a * sin], axis=1)
        o_ref[:, c:c + HGW] = out.astype(jnp.bfloat16)


def _proj(x, g, w, bg, cos4, sin4):
    B, S, _ = x.shape
    return pl.pallas_call(
        _proj_kernel,
        name="proj",
        grid=(B, S // TM),
        in_specs=[
            pl.BlockSpec((None, TM, D_MODEL), lambda b, i: (b, i, 0)),
            _const_spec((1, D_MODEL)),
            _const_spec((D_MODEL, PROJ_W)),
            _const_spec((1, GATE_WIDTH)),
            pl.BlockSpec((None, TM, LANES), lambda b, i: (b, i, 0)),
            pl.BlockSpec((None, TM, LANES), lambda b, i: (b, i, 0)),
        ],
        out_specs=pl.BlockSpec((None, TM, PROJ_W), lambda b, i: (b, i, 0)),
        out_shape=jax.ShapeDtypeStruct((B, S, PROJ_W), jnp.bfloat16),
        compiler_params=_params(2),
    )(x, g, w, bg, cos4, sin4)


def _attn_kernel(sink_ref, q_ref, kc_ref, kp_ref, vc_ref, vp_ref, o_ref, *rest,
                 has_sink, write_lse):
    lse_ref = rest[0] if write_lse else None
    first = pl.program_id(1) == 0
    nqb = QT // BLOCK

    row = lax.broadcasted_iota(jnp.int32, (BLOCK, 2 * BLOCK), 0)
    col = lax.broadcasted_iota(jnp.int32, (BLOCK, 2 * BLOCK), 1)
    band = (col >= row) & (col <= row + BLOCK)
    bias = jnp.where(band, 0.0, NEG).astype(jnp.float32)
    bias_first = jnp.where(band & ((col >= BLOCK) | jnp.logical_not(first)), 0.0, NEG
                           ).astype(jnp.float32)

    lane = lax.broadcasted_iota(jnp.int32, (1, HGW), 1)
    qmask = [jnp.where((lane % LANES) // HALF == j, 1.0, 0.0).astype(jnp.bfloat16)
             for j in range(HG)]
    vmask = [jnp.where(lane // HEAD_DIM == j, 1.0, 0.0).astype(jnp.bfloat16)
             for j in range(HG)]
    lse_lane = lax.broadcasted_iota(jnp.int32, (BLOCK, LANES), 1)

    lse_tiles = [jnp.zeros((BLOCK, LANES), jnp.float32) for _ in range(nqb)]
    for g in range(2):
        cs = slice(g * HGW, (g + 1) * HGW)
        kall = jnp.concatenate([kp_ref[:, cs], kc_ref[:, cs]], axis=0)
        vall = jnp.concatenate([vp_ref[:, cs], vc_ref[:, cs]], axis=0)
        vms = [vall * vmask[j] for j in range(HG)]
        for qb in range(nqb):
            q4 = q_ref[qb * BLOCK:(qb + 1) * BLOCK, cs]
            kcat = kall[qb * BLOCK:(qb + 2) * BLOCK]
            bias_q = bias_first if qb == 0 else bias
            ps = []
            for j in range(HG):
                s = lax.dot_general(q4 * qmask[j], kcat, (((1,), (1,)), ((), ())),
                                    preferred_element_type=jnp.float32) + bias_q
                m = jnp.max(s, axis=-1, keepdims=True)
                if has_sink:
                    sk = sink_ref[g * HG + j]
                    m = jnp.maximum(m, sk)
                p = jnp.exp(s - m)
                den = jnp.sum(p, axis=-1, keepdims=True)
                if has_sink:
                    den = den + jnp.exp(sk - m)
                ps.append((p * (1.0 / den)).astype(jnp.bfloat16))
                if write_lse:
                    lse = m + jnp.log(den)
                    lse_tiles[qb] = jnp.where(lse_lane == g * HG + j, lse, lse_tiles[qb])
            p4 = jnp.concatenate(ps, axis=1)
            vbd = jnp.concatenate([vm[qb * BLOCK:(qb + 2) * BLOCK] for vm in vms], axis=0)
            o4 = jnp.dot(p4, vbd, preferred_element_type=jnp.float32)
            o_ref[qb * BLOCK:(qb + 1) * BLOCK, cs] = o4.astype(jnp.bfloat16)
    if write_lse:
        for qb in range(nqb):
            lse_ref[qb * BLOCK:(qb + 1) * BLOCK, :] = lse_tiles[qb][:, :2 * HG]


def _attn(q_arr, k_arr, v_arr, q_cb, k_cb, v_cb, sink, *, has_sink, write_lse):
    N, L, _ = q_arr.shape
    r = QT // BLOCK
    cur = lambda cb: pl.BlockSpec((None, QT, GROUP_W), lambda n, i: (n, i, cb))
    prev = lambda cb: pl.BlockSpec((None, BLOCK, GROUP_W),
                                   lambda n, i: (n, jnp.maximum(i * r - 1, 0), cb))
    out_specs = [pl.BlockSpec((None, QT, GROUP_W), lambda n, i: (n, i, 0))]
    out_shape = [jax.ShapeDtypeStruct((N, L, GROUP_W), jnp.bfloat16)]
    if write_lse:
        out_specs.append(pl.BlockSpec((None, QT, 2 * HG), lambda n, i: (n, i, 0)))
        out_shape.append(jax.ShapeDtypeStruct((N, L, 2 * HG), jnp.float32))
    res = pl.pallas_call(
        functools.partial(_attn_kernel, has_sink=has_sink, write_lse=write_lse),
        name="attn_sink" if has_sink else "attn",
        grid=(N, L // QT),
        in_specs=[pl.BlockSpec(memory_space=pltpu.SMEM),
                  cur(q_cb), cur(k_cb), prev(k_cb), cur(v_cb), prev(v_cb)],
        out_specs=out_specs,
        out_shape=out_shape,
        compiler_params=_params(2),
    )(sink, q_arr, k_arr, k_arr, v_arr, v_arr)
    return res if write_lse else res[0]


def _merge_kernel(x_ref, o0_ref, o1_ref, o2_ref, l0_ref, l1_ref, l2_ref, ob_ref, gt_ref,
                  wa_ref, wb_ref, wo_ref, out_ref):
    ls = [l0_ref[...], l1_ref[...], l2_ref[...]]
    mx = jnp.maximum(jnp.maximum(ls[0], ls[1]), ls[2])
    es = [jnp.exp(l - mx) for l in ls]
    inv = 1.0 / (es[0] + es[1] + es[2])
    lane = lax.broadcasted_iota(jnp.int32, (1, GROUP_W), 1) // HEAD_DIM
    oa = jnp.zeros((TM, GROUP_W), jnp.float32)
    for e, o_ref in zip(es, (o0_ref, o1_ref, o2_ref)):
        alpha = e * inv
        wide = jnp.zeros((TM, GROUP_W), jnp.float32)
        for hh in range(DIL_HEADS):
            wide = jnp.where(lane == hh, alpha[:, hh:hh + 1], wide)
        oa = oa + wide * o_ref[...].astype(jnp.float32)
    ya = jnp.dot(oa.astype(jnp.bfloat16), wa_ref[...], preferred_element_type=jnp.float32)
    yb = jnp.dot(ob_ref[...], wb_ref[...], preferred_element_type=jnp.float32)
    merged = (gt_ref[:, :D_MODEL].astype(jnp.float32) * ya
              + gt_ref[:, D_MODEL:].astype(jnp.float32) * yb)
    out_ref[...] = x_ref[...] + jnp.dot(merged.astype(jnp.bfloat16), wo_ref[...],
                                        preferred_element_type=jnp.float32)


def _merge(x, proj, o_dil, lse_dil, o_swa, wa, wb, wo):
    B, S, _ = x.shape
    tok = lambda w, cb=0: pl.BlockSpec((None, TM, w), lambda b, i: (b, i, cb))
    return pl.pallas_call(
        _merge_kernel,
        name="merge",
        grid=(B, S // TM),
        in_specs=[tok(D_MODEL)] + [tok(GROUP_W)] * 3 + [tok(DIL_HEADS)] * 3
                 + [tok(GROUP_W), tok(GATE_WIDTH, COL_GATE // GATE_WIDTH),
                    _const_spec((DIL_OUT, D_MODEL)), _const_spec((SWA_Q_WIDTH, D_MODEL)),
                    _const_spec((D_MODEL, D_MODEL))],
        out_specs=tok(D_MODEL),
        out_shape=jax.ShapeDtypeStruct((B, S, D_MODEL), jnp.float32),
        compiler_params=_params(2),
    )(x, *o_dil, *lse_dil, o_swa, proj, wa, wb, wo)


def _memkv_kernel(mem_ref, g_ref, w_ref, kv_ref):
    mn = _rms(mem_ref[...], g_ref[...]).astype(jnp.bfloat16)
    kv_ref[...] = jnp.dot(mn, w_ref[...], preferred_element_type=jnp.float32
                          ).astype(jnp.bfloat16)


def _memkv(mem, g, w):
    B, M, _ = mem.shape
    return pl.pallas_call(
        _memkv_kernel,
        name="memkv",
        grid=(B,),
        in_specs=[pl.BlockSpec((None, M, D_MODEL), lambda b: (b, 0, 0)),
                  _const_spec((1, D_MODEL)), _const_spec((D_MODEL, 2 * D_MODEL))],
        out_specs=pl.BlockSpec((None, M, 2 * D_MODEL), lambda b: (b, 0, 0)),
        out_shape=jax.ShapeDtypeStruct((B, M, 2 * D_MODEL), jnp.bfloat16),
        compiler_params=_params(1),
    )(mem, g, w)


def _cross_kernel(x_ref, g_ref, wq_ref, kv_ref, wo_ref, out_ref):
    x = x_ref[...]
    hc = _rms(x, g_ref[...]).astype(jnp.bfloat16)
    q = jnp.dot(hc, wq_ref[...], preferred_element_type=jnp.float32).astype(jnp.bfloat16)
    os = []
    for hh in range(X_HEADS):
        cs = slice(hh * X_HEAD_DIM, (hh + 1) * X_HEAD_DIM)
        s = lax.dot_general(q[:, cs], kv_ref[:, cs], (((1,), (1,)), ((), ())),
                            preferred_element_type=jnp.float32)
        m = jnp.max(s, axis=-1, keepdims=True)
        p = jnp.exp(s - m)
        p = p * (1.0 / jnp.sum(p, axis=-1, keepdims=True))
        vs = slice(D_MODEL + hh * X_HEAD_DIM, D_MODEL + (hh + 1) * X_HEAD_DIM)
        os.append(jnp.dot(p.astype(jnp.bfloat16), kv_ref[:, vs],
                          preferred_element_type=jnp.float32).astype(jnp.bfloat16))
    o = jnp.concatenate(os, axis=1)
    out_ref[...] = x + jnp.dot(o, wo_ref[...], preferred_element_type=jnp.float32)


def _cross(x, g, wq, kv, wo):
    B, S, _ = x.shape
    M = kv.shape[1]
    tok = pl.BlockSpec((None, TM, D_MODEL), lambda b, i: (b, i, 0))
    return pl.pallas_call(
        _cross_kernel,
        name="cross",
        grid=(B, S // TM),
        in_specs=[tok, _const_spec((1, D_MODEL)), _const_spec((D_MODEL, D_MODEL)),
                  pl.BlockSpec((None, M, 2 * D_MODEL), lambda b, i: (b, 0, 0)),
                  _const_spec((D_MODEL, D_MODEL))],
        out_specs=tok,
        out_shape=jax.ShapeDtypeStruct((B, S, D_MODEL), jnp.float32),
        compiler_params=_params(2),
    )(x, g, wq, kv, wo)


FF_CHUNK = 1024


def _mlp_kernel(x_ref, g_ref, w1_ref, w2_ref, gf_ref, out_ref, *, final_norm):
    x = x_ref[...]
    h = _rms(x, g_ref[...]).astype(jnp.bfloat16)
    y = x
    for c in range(0, D_FF, FF_CHUNK):
        a = jnp.maximum(jnp.dot(h, w1_ref[:, c:c + FF_CHUNK],
                                preferred_element_type=jnp.float32), 0.0)
        y = y + jnp.dot((a * a).astype(jnp.bfloat16), w2_ref[c:c + FF_CHUNK, :],
                        preferred_element_type=jnp.float32)
    out_ref[...] = _rms(y, gf_ref[...]) if final_norm else y


def _mlp(x, g, w1, w2, gf, final_norm):
    B, S, _ = x.shape
    tok = pl.BlockSpec((None, TM, D_MODEL), lambda b, i: (b, i, 0))
    return pl.pallas_call(
        functools.partial(_mlp_kernel, final_norm=final_norm),
        name="mlp",
        grid=(B, S // TM),
        in_specs=[tok, _const_spec((1, D_MODEL)), _const_spec((D_MODEL, D_FF)),
                  _const_spec((D_FF, D_MODEL)), _const_spec((1, D_MODEL))],
        out_specs=tok,
        out_shape=jax.ShapeDtypeStruct((B, S, D_MODEL), jnp.float32),
        compiler_params=_params(2),
    )(x, g, w1, w2, gf)


def _split_half(w, n_heads):
    k = w.shape[0]
    w = w.reshape(k, n_heads // HG, HG, 2, HALF)
    return w.transpose(0, 1, 3, 2, 4).reshape(k, n_heads * HEAD_DIM)


def _permute_w_in(w_in):
    scale = HEAD_DIM ** -0.5
    c = 0
    qa = w_in[:, c:c + DIL_WIDTH]; c += DIL_WIDTH
    ka = w_in[:, c:c + DIL_WIDTH]; c += DIL_WIDTH
    va = w_in[:, c:c + DIL_WIDTH]; c += DIL_WIDTH
    qb = w_in[:, c:c + SWA_Q_WIDTH]; c += SWA_Q_WIDTH
    kb = w_in[:, c:c + SWA_KV_WIDTH]; c += SWA_KV_WIDTH
    vb = w_in[:, c:c + SWA_KV_WIDTH]; c += SWA_KV_WIDTH
    gt = w_in[:, c:]
    cols = []
    for g in range(N_DIL):
        sl = slice(g * GROUP_W, (g + 1) * GROUP_W)
        cols += [_split_half(qa[:, sl] * scale, DIL_HEADS), _split_half(ka[:, sl], DIL_HEADS),
                 va[:, sl]]
    cols.append(_split_half(qb * scale, SWA_Q_HEADS))
    kd = w_in.shape[0]
    kb = kb.reshape(kd, SWA_KV_HEADS, 2, 1, HALF)
    cols.append(jnp.broadcast_to(kb, (kd, SWA_KV_HEADS, 2, HG, HALF)).reshape(kd, GROUP_W))
    vb = vb.reshape(kd, SWA_KV_HEADS, 1, HEAD_DIM)
    cols.append(jnp.broadcast_to(vb, (kd, SWA_KV_HEADS, HG, HEAD_DIM)).reshape(kd, GROUP_W))
    cols.append(gt)
    return jnp.concatenate(cols, axis=1).astype(jnp.bfloat16)


def _to_res(a, d):
    B, S, W = a.shape
    return a.reshape(B, S // d, d, W).swapaxes(1, 2).reshape(B * d, S // d, W)


def _from_res(a, d, B):
    _, L, W = a.shape
    return a.reshape(B, d, L, W).swapaxes(1, 2).reshape(B, L * d, W)


def kernel(x, mem, positions, g_mix, w_in, b_gate, sink, w_branch_a, w_branch_b, w_out,
           g_cross, g_mem, w_cq, w_ckv, w_co, g_mlp, w_1, w_2, g_final):
    bf = jnp.bfloat16
    B, S, _ = x.shape
    depth = g_mix.shape[0]
    inv = ROPE_THETA ** (-jnp.arange(HALF, dtype=jnp.float32) / HALF)
    ang = positions.astype(jnp.float32)[..., None] * inv
    cos4 = jnp.tile(jnp.cos(ang), (1, 1, HG))
    sin4 = jnp.tile(jnp.sin(ang), (1, 1, HG))
    for l in range(depth):
        proj = _proj(x, g_mix[l][None], _permute_w_in(w_in[l]),
                     b_gate[l].reshape(1, GATE_WIDTH), cos4, sin4)
        o_dil, lse_dil = [], []
        for gi, (_, d) in enumerate(DIL_GROUPS):
            if d == 1:
                o, lse = _attn(proj, proj, proj, 3 * gi, 3 * gi + 1, 3 * gi + 2, sink[l],
                               has_sink=False, write_lse=True)
            else:
                c0 = gi * 3 * GROUP_W
                qkv = [_to_res(proj[:, :, c0 + t * GROUP_W:c0 + (t + 1) * GROUP_W], d)
                       for t in range(3)]
                o, lse = _attn(*qkv, 0, 0, 0, sink[l], has_sink=False, write_lse=True)
                o, lse = _from_res(o, d, B), _from_res(lse, d, B)
            o_dil.append(o)
            lse_dil.append(lse)
        o_swa = _attn(proj, proj, proj, COL_SWA_Q // GROUP_W, COL_SWA_K // GROUP_W,
                      COL_SWA_V // GROUP_W, sink[l], has_sink=True, write_lse=False)
        x = _merge(x, proj, o_dil, lse_dil, o_swa, w_branch_a[l].astype(bf),
                   w_branch_b[l].astype(bf), w_out[l].astype(bf))
        kv = _memkv(mem, g_mem[l][None], w_ckv[l].astype(bf))
        x = _cross(x, g_cross[l][None], (w_cq[l] * (X_HEAD_DIM ** -0.5)).astype(bf), kv,
                   w_co[l].astype(bf))
        x = _mlp(x, g_mlp[l][None], w_1[l].astype(bf), w_2[l].astype(bf), g_final[None],
                 final_norm=(l == depth - 1))
    return x
```

```python
import functools

import jax
import jax.numpy as jnp
from jax import lax
from jax.experimental import pallas as pl
from jax.experimental.pallas import tpu as pltpu

D_MODEL = 1024
HEAD_DIM = 64
HALF = HEAD_DIM // 2
DIL_GROUPS = ((128, 1), (512, 4), (2048, 16))
N_DIL = len(DIL_GROUPS)
DIL_HEADS = 8
SWA_Q_HEADS = 8
SWA_KV_HEADS = 2
BLOCK = 128
ROPE_THETA = 10000.0
X_HEADS = 4
X_HEAD_DIM = D_MODEL // X_HEADS
D_FF = 4 * D_MODEL
EPS = 1e-6
N_BRANCH = 2

DIL_WIDTH = N_DIL * DIL_HEADS * HEAD_DIM
DIL_OUT = DIL_HEADS * HEAD_DIM
SWA_Q_WIDTH = SWA_Q_HEADS * HEAD_DIM
SWA_KV_WIDTH = SWA_KV_HEADS * HEAD_DIM
GATE_WIDTH = N_BRANCH * D_MODEL

LANES = 128
HG = 4
HGW = HG * HEAD_DIM
GROUP_W = 2 * HGW

COL_DIL = 0
COL_SWA_Q = N_DIL * 3 * GROUP_W
COL_SWA_K = COL_SWA_Q + GROUP_W
COL_SWA_V = COL_SWA_K + GROUP_W
COL_GATE = COL_SWA_V + GROUP_W
PROJ_W = COL_GATE + GATE_WIDTH

TM = 512
QT = 512
NEG = -1e30
VMEM_LIMIT = 56 * 1024 * 1024


def _const_spec(shape):
    nd = len(shape)
    return pl.BlockSpec(shape, lambda *_: (0,) * nd, pipeline_mode=pl.Buffered(1))


def _params(n_axes):
    return pltpu.CompilerParams(dimension_semantics=("parallel",) * n_axes,
                                vmem_limit_bytes=VMEM_LIMIT)


def _rms(x, g):
    ms = jnp.mean(x * x, axis=-1, keepdims=True)
    return x * lax.rsqrt(ms + EPS) * g


def _proj_kernel(x_ref, g_ref, w_ref, bg_ref, cos_ref, sin_ref, o_ref):
    h = _rms(x_ref[...], g_ref[...]).astype(jnp.bfloat16)
    cos = cos_ref[...]
    sin = sin_ref[...]
    for c in range(0, PROJ_W, HGW):
        r = jnp.dot(h, w_ref[:, c:c + HGW], preferred_element_type=jnp.float32)
        if c >= COL_GATE:
            z = r + bg_ref[:, c - COL_GATE:c - COL_GATE + HGW]
            out = 1.0 / (1.0 + jnp.exp(-z))
        elif c >= COL_SWA_V or (c < COL_SWA_Q and (c % (3 * GROUP_W)) >= 2 * GROUP_W):
            out = r
        else:
            a = r[:, :LANES]
            b = r[:, LANES:]
            out = jnp.concatenate([a * cos - b * sin, b * cos + a * sin], axis=1)
        o_ref[:, c:c + HGW] = out.astype(jnp.bfloat16)


def _proj(x, g, w, bg, cos4, sin4):
    B, S, _ = x.shape
    return pl.pallas_call(
        _proj_kernel,
        name="proj",
        grid=(B, S // TM),
        in_specs=[
            pl.BlockSpec((None, TM, D_MODEL), lambda b, i: (b, i, 0)),
            _const_spec((1, D_MODEL)),
            _const_spec((D_MODEL, PROJ_W)),
            _const_spec((1, GATE_WIDTH)),
            pl.BlockSpec((None, TM, LANES), lambda b, i: (b, i, 0)),
            pl.BlockSpec((None, TM, LANES), lambda b, i: (b, i, 0)),
        ],
        out_specs=pl.BlockSpec((None, TM, PROJ_W), lambda b, i: (b, i, 0)),
        out_shape=jax.ShapeDtypeStruct((B, S, PROJ_W), jnp.bfloat16),
        compiler_params=_params(2),
    )(x, g, w, bg, cos4, sin4)


def _attn_kernel(sink_ref, q_ref, kc_ref, kp_ref, vc_ref, vp_ref, o_ref, *rest,
                 has_sink, write_lse):
    lse_ref = rest[0] if write_lse else None
    first = pl.program_id(1) == 0
    nqb = QT // BLOCK

    row = lax.broadcasted_iota(jnp.int32, (BLOCK, 2 * BLOCK), 0)
    col = lax.broadcasted_iota(jnp.int32, (BLOCK, 2 * BLOCK), 1)
    band = (col >= row) & (col <= row + BLOCK)
    bias = jnp.where(band, 0.0, NEG).astype(jnp.float32)
    bias_first = jnp.where(band & ((col >= BLOCK) | jnp.logical_not(first)), 0.0, NEG
                           ).astype(jnp.float32)

    lane = lax.broadcasted_iota(jnp.int32, (1, HGW), 1)
    qmask = [jnp.where((lane % LANES) // HALF == j, 1.0, 0.0).astype(jnp.bfloat16)
             for j in range(HG)]
    vhead = lane // HEAD_DIM
    lse_lane = lax.broadcasted_iota(jnp.int32, (BLOCK, LANES), 1)

    lse_tiles = [jnp.zeros((BLOCK, LANES), jnp.float32) for _ in range(nqb)]
    for g in range(2):
        cs = slice(g * HGW, (g + 1) * HGW)
        kall = jnp.concatenate([kp_ref[:, cs], kc_ref[:, cs]], axis=0)
        vall = jnp.concatenate([vp_ref[:, cs], vc_ref[:, cs]], axis=0)
        for qb in range(nqb):
            q4 = q_ref[qb * BLOCK:(qb + 1) * BLOCK, cs]
            kcat = kall[qb * BLOCK:(qb + 2) * BLOCK]
            vcat = vall[qb * BLOCK:(qb + 2) * BLOCK]
            bias_q = bias_first if qb == 0 else bias
            qs = jnp.concatenate([q4 * qmask[j] for j in range(HG)], axis=0)
            s_all = lax.dot_general(qs, kcat, (((1,), (1,)), ((), ())),
                                    preferred_element_type=jnp.float32)
            ps = []
            for j in range(HG):
                s = s_all[j * BLOCK:(j + 1) * BLOCK] + bias_q
                m = jnp.max(s, axis=-1, keepdims=True)
                if has_sink:
                    sk = sink_ref[g * HG + j]
                    m = jnp.maximum(m, sk)
                p = jnp.exp(s - m)
                den = jnp.sum(p, axis=-1, keepdims=True)
                if has_sink:
                    den = den + jnp.exp(sk - m)
                ps.append((p * (1.0 / den)).astype(jnp.bfloat16))
                if write_lse:
                    lse = m + jnp.log(den)
                    lse_tiles[qb] = jnp.where(lse_lane == g * HG + j, lse, lse_tiles[qb])
            o_s = jnp.dot(jnp.concatenate(ps, axis=0), vcat,
                          preferred_element_type=jnp.float32)
            o4 = o_s[:BLOCK]
            for j in range(1, HG):
                o4 = jnp.where(vhead == j, o_s[j * BLOCK:(j + 1) * BLOCK], o4)
            o_ref[qb * BLOCK:(qb + 1) * BLOCK, cs] = o4.astype(jnp.bfloat16)
    if write_lse:
        for qb in range(nqb):
            lse_ref[qb * BLOCK:(qb + 1) * BLOCK, :] = lse_tiles[qb][:, :2 * HG]


def _attn(q_arr, k_arr, v_arr, q_cb, k_cb, v_cb, sink, *, has_sink, write_lse):
    N, L, _ = q_arr.shape
    r = QT // BLOCK
    cur = lambda cb: pl.BlockSpec((None, QT, GROUP_W), lambda n, i: (n, i, cb))
    prev = lambda cb: pl.BlockSpec((None, BLOCK, GROUP_W),
                                   lambda n, i: (n, jnp.maximum(i * r - 1, 0), cb))
    out_specs = [pl.BlockSpec((None, QT, GROUP_W), lambda n, i: (n, i, 0))]
    out_shape = [jax.ShapeDtypeStruct((N, L, GROUP_W), jnp.bfloat16)]
    if write_lse:
        out_specs.append(pl.BlockSpec((None, QT, 2 * HG), lambda n, i: (n, i, 0)))
        out_shape.append(jax.ShapeDtypeStruct((N, L, 2 * HG), jnp.float32))
    res = pl.pallas_call(
        functools.partial(_attn_kernel, has_sink=has_sink, write_lse=write_lse),
        name="attn_sink" if has_sink else "attn",
        grid=(N, L // QT),
        in_specs=[pl.BlockSpec(memory_space=pltpu.SMEM),
                  cur(q_cb), cur(k_cb), prev(k_cb), cur(v_cb), prev(v_cb)],
        out_specs=out_specs,
        out_shape=out_shape,
        compiler_params=_params(2),
    )(sink, q_arr, k_arr, k_arr, v_arr, v_arr)
    return res if write_lse else res[0]


def _merge_kernel(x_ref, o0_ref, o1_ref, o2_ref, l0_ref, l1_ref, l2_ref, ob_ref, gt_ref,
                  wa_ref, wb_ref, wo_ref, out_ref):
    ls = [l0_ref[...], l1_ref[...], l2_ref[...]]
    mx = jnp.maximum(jnp.maximum(ls[0], ls[1]), ls[2])
    es = [jnp.exp(l - mx) for l in ls]
    inv = 1.0 / (es[0] + es[1] + es[2])
    lane = lax.broadcasted_iota(jnp.int32, (1, GROUP_W), 1) // HEAD_DIM
    oa = jnp.zeros((TM, GROUP_W), jnp.float32)
    for e, o_ref in zip(es, (o0_ref, o1_ref, o2_ref)):
        alpha = e * inv
        wide = jnp.zeros((TM, GROUP_W), jnp.float32)
        for hh in range(DIL_HEADS):
            wide = jnp.where(lane == hh, alpha[:, hh:hh + 1], wide)
        oa = oa + wide * o_ref[...].astype(jnp.float32)
    ya = jnp.dot(oa.astype(jnp.bfloat16), wa_ref[...], preferred_element_type=jnp.float32)
    yb = jnp.dot(ob_ref[...], wb_ref[...], preferred_element_type=jnp.float32)
    merged = (gt_ref[:, :D_MODEL].astype(jnp.float32) * ya
              + gt_ref[:, D_MODEL:].astype(jnp.float32) * yb)
    out_ref[...] = x_ref[...] + jnp.dot(merged.astype(jnp.bfloat16), wo_ref[...],
                                        preferred_element_type=jnp.float32)


def _merge(x, proj, o_dil, lse_dil, o_swa, wa, wb, wo):
    B, S, _ = x.shape
    tok = lambda w, cb=0: pl.BlockSpec((None, TM, w), lambda b, i: (b, i, cb))
    return pl.pallas_call(
        _merge_kernel,
        name="merge",
        grid=(B, S // TM),
        in_specs=[tok(D_MODEL)] + [tok(GROUP_W)] * 3 + [tok(DIL_HEADS)] * 3
                 + [tok(GROUP_W), tok(GATE_WIDTH, COL_GATE // GATE_WIDTH),
                    _const_spec((DIL_OUT, D_MODEL)), _const_spec((SWA_Q_WIDTH, D_MODEL)),
                    _const_spec((D_MODEL, D_MODEL))],
        out_specs=tok(D_MODEL),
        out_shape=jax.ShapeDtypeStruct((B, S, D_MODEL), jnp.float32),
        compiler_params=_params(2),
    )(x, *o_dil, *lse_dil, o_swa, proj, wa, wb, wo)


def _memkv_kernel(mem_ref, g_ref, w_ref, kv_ref):
    mn = _rms(mem_ref[...], g_ref[...]).astype(jnp.bfloat16)
    kv_ref[...] = jnp.dot(mn, w_ref[...], preferred_element_type=jnp.float32
                          ).astype(jnp.bfloat16)


def _memkv(mem, g, w):
    B, M, _ = mem.shape
    return pl.pallas_call(
        _memkv_kernel,
        name="memkv",
        grid=(B,),
        in_specs=[pl.BlockSpec((None, M, D_MODEL), lambda b: (b, 0, 0)),
                  _const_spec((1, D_MODEL)), _const_spec((D_MODEL, 2 * D_MODEL))],
        out_specs=pl.BlockSpec((None, M, 2 * D_MODEL), lambda b: (b, 0, 0)),
        out_shape=jax.ShapeDtypeStruct((B, M, 2 * D_MODEL), jnp.bfloat16),
        compiler_params=_params(1),
    )(mem, g, w)


def _cross_kernel(x_ref, g_ref, wq_ref, kv_ref, wo_ref, out_ref):
    x = x_ref[...]
    hc = _rms(x, g_ref[...]).astype(jnp.bfloat16)
    q = jnp.dot(hc, wq_ref[...], preferred_element_type=jnp.float32).astype(jnp.bfloat16)
    os = []
    for hh in range(X_HEADS):
        cs = slice(hh * X_HEAD_DIM, (hh + 1) * X_HEAD_DIM)
        s = lax.dot_general(q[:, cs], kv_ref[:, cs], (((1,), (1,)), ((), ())),
                            preferred_element_type=jnp.float32)
        m = jnp.max(s, axis=-1, keepdims=True)
        p = jnp.exp(s - m)
        p = p * (1.0 / jnp.sum(p, axis=-1, keepdims=True))
        vs = slice(D_MODEL + hh * X_HEAD_DIM, D_MODEL + (hh + 1) * X_HEAD_DIM)
        os.append(jnp.dot(p.astype(jnp.bfloat16), kv_ref[:, vs],
                          preferred_element_type=jnp.float32).astype(jnp.bfloat16))
    o = jnp.concatenate(os, axis=1)
    out_ref[...] = x + jnp.dot(o, wo_ref[...], preferred_element_type=jnp.float32)


def _cross(x, g, wq, kv, wo):
    B, S, _ = x.shape
    M = kv.shape[1]
    tok = pl.BlockSpec((None, TM, D_MODEL), lambda b, i: (b, i, 0))
    return pl.pallas_call(
        _cross_kernel,
        name="cross",
        grid=(B, S // TM),
        in_specs=[tok, _const_spec((1, D_MODEL)), _const_spec((D_MODEL, D_MODEL)),
                  pl.BlockSpec((None, M, 2 * D_MODEL), lambda b, i: (b, 0, 0)),
                  _const_spec((D_MODEL, D_MODEL))],
        out_specs=tok,
        out_shape=jax.ShapeDtypeStruct((B, S, D_MODEL), jnp.float32),
        compiler_params=_params(2),
    )(x, g, wq, kv, wo)


FF_CHUNK = 1024


def _mlp_kernel(x_ref, g_ref, w1_ref, w2_ref, gf_ref, out_ref, *, final_norm):
    x = x_ref[...]
    h = _rms(x, g_ref[...]).astype(jnp.bfloat16)
    y = x
    for c in range(0, D_FF, FF_CHUNK):
        a = jnp.maximum(jnp.dot(h, w1_ref[:, c:c + FF_CHUNK],
                                preferred_element_type=jnp.float32), 0.0)
        y = y + jnp.dot((a * a).astype(jnp.bfloat16), w2_ref[c:c + FF_CHUNK, :],
                        preferred_element_type=jnp.float32)
    out_ref[...] = _rms(y, gf_ref[...]) if final_norm else y


def _mlp(x, g, w1, w2, gf, final_norm):
    B, S, _ = x.shape
    tok = pl.BlockSpec((None, TM, D_MODEL), lambda b, i: (b, i, 0))
    return pl.pallas_call(
        functools.partial(_mlp_kernel, final_norm=final_norm),
        name="mlp",
        grid=(B, S // TM),
        in_specs=[tok, _const_spec((1, D_MODEL)), _const_spec((D_MODEL, D_FF)),
                  _const_spec((D_FF, D_MODEL)), _const_spec((1, D_MODEL))],
        out_specs=tok,
        out_shape=jax.ShapeDtypeStruct((B, S, D_MODEL), jnp.float32),
        compiler_params=_params(2),
    )(x, g, w1, w2, gf)


def _split_half(w, n_heads):
    k = w.shape[0]
    w = w.reshape(k, n_heads // HG, HG, 2, HALF)
    return w.transpose(0, 1, 3, 2, 4).reshape(k, n_heads * HEAD_DIM)


def _permute_w_in(w_in):
    scale = HEAD_DIM ** -0.5
    c = 0
    qa = w_in[:, c:c + DIL_WIDTH]; c += DIL_WIDTH
    ka = w_in[:, c:c + DIL_WIDTH]; c += DIL_WIDTH
    va = w_in[:, c:c + DIL_WIDTH]; c += DIL_WIDTH
    qb = w_in[:, c:c + SWA_Q_WIDTH]; c += SWA_Q_WIDTH
    kb = w_in[:, c:c + SWA_KV_WIDTH]; c += SWA_KV_WIDTH
    vb = w_in[:, c:c + SWA_KV_WIDTH]; c += SWA_KV_WIDTH
    gt = w_in[:, c:]
    cols = []
    for g in range(N_DIL):
        sl = slice(g * GROUP_W, (g + 1) * GROUP_W)
        cols += [_split_half(qa[:, sl] * scale, DIL_HEADS), _split_half(ka[:, sl], DIL_HEADS),
                 va[:, sl]]
    cols.append(_split_half(qb * scale, SWA_Q_HEADS))
    kd = w_in.shape[0]
    kb = kb.reshape(kd, SWA_KV_HEADS, 2, 1, HALF)
    cols.append(jnp.broadcast_to(kb, (kd, SWA_KV_HEADS, 2, HG, HALF)).reshape(kd, GROUP_W))
    vb = vb.reshape(kd, SWA_KV_HEADS, 1, HEAD_DIM)
    cols.append(jnp.broadcast_to(vb, (kd, SWA_KV_HEADS, HG, HEAD_DIM)).reshape(kd, GROUP_W))
    cols.append(gt)
    return jnp.concatenate(cols, axis=1).astype(jnp.bfloat16)


def _to_res(a, d):
    B, S, W = a.shape
    return a.reshape(B, S // d, d, W).swapaxes(1, 2).reshape(B * d, S // d, W)


def _from_res(a, d, B):
    _, L, W = a.shape
    return a.reshape(B, d, L, W).swapaxes(1, 2).reshape(B, L * d, W)


def kernel(x, mem, positions, g_mix, w_in, b_gate, sink, w_branch_a, w_branch_b, w_out,
           g_cross, g_mem, w_cq, w_ckv, w_co, g_mlp, w_1, w_2, g_final):
    bf = jnp.bfloat16
    B, S, _ = x.shape
    depth = g_mix.shape[0]
    inv = ROPE_THETA ** (-jnp.arange(HALF, dtype=jnp.float32) / HALF)
    ang = positions.astype(jnp.float32)[..., None] * inv
    cos4 = jnp.tile(jnp.cos(ang), (1, 1, HG))
    sin4 = jnp.tile(jnp.sin(ang), (1, 1, HG))
    for l in range(depth):
        proj = _proj(x, g_mix[l][None], _permute_w_in(w_in[l]),
                     b_gate[l].reshape(1, GATE_WIDTH), cos4, sin4)
        o_dil, lse_dil = [], []
        for gi, (_, d) in enumerate(DIL_GROUPS):
            if d == 1:
                o, lse = _attn(proj, proj, proj, 3 * gi, 3 * gi + 1, 3 * gi + 2, sink[l],
                               has_sink=False, write_lse=True)
            else:
                c0 = gi * 3 * GROUP_W
                qkv = [_to_res(proj[:, :, c0 + t * GROUP_W:c0 + (t + 1) * GROUP_W], d)
                       for t in range(3)]
                o, lse = _attn(*qkv, 0, 0, 0, sink[l], has_sink=False, write_lse=True)
                o, lse = _from_res(o, d, B), _from_res(lse, d, B)
            o_dil.append(o)
            lse_dil.append(lse)
        o_swa = _attn(proj, proj, proj, COL_SWA_Q // GROUP_W, COL_SWA_K // GROUP_W,
                      COL_SWA_V // GROUP_W, sink[l], has_sink=True, write_lse=False)
        x = _merge(x, proj, o_dil, lse_dil, o_swa, w_branch_a[l].astype(bf),
                   w_branch_b[l].astype(bf), w_out[l].astype(bf))
        kv = _memkv(mem, g_mem[l][None], w_ckv[l].astype(bf))
        x = _cross(x, g_cross[l][None], (w_cq[l] * (X_HEAD_DIM ** -0.5)).astype(bf), kv,
                   w_co[l].astype(bf))
        x = _mlp(x, g_mlp[l][None], w_1[l].astype(bf), w_2[l].astype(bf), g_final[None],
                 final_norm=(l == depth - 1))
    return x
```

```python
import functools

import numpy as np
import jax
import jax.numpy as jnp
from jax import lax
from jax.experimental import pallas as pl
from jax.experimental.pallas import tpu as pltpu

D_MODEL = 1024
HEAD_DIM = 64
HALF = HEAD_DIM // 2
DIL_GROUPS = ((128, 1), (512, 4), (2048, 16))
N_DIL = len(DIL_GROUPS)
DIL_HEADS = 8
SWA_Q_HEADS = 8
SWA_KV_HEADS = 2
BLOCK = 128
ROPE_THETA = 10000.0
X_HEADS = 4
X_HEAD_DIM = D_MODEL // X_HEADS
D_FF = 4 * D_MODEL
EPS = 1e-6
N_BRANCH = 2

DIL_WIDTH = N_DIL * DIL_HEADS * HEAD_DIM
DIL_OUT = DIL_HEADS * HEAD_DIM
SWA_Q_WIDTH = SWA_Q_HEADS * HEAD_DIM
SWA_KV_WIDTH = SWA_KV_HEADS * HEAD_DIM
GATE_WIDTH = N_BRANCH * D_MODEL

LANES = 128
HG = 4
HGW = HG * HEAD_DIM
GROUP_W = 2 * HGW
QKV_W = 3 * GROUP_W

COL_GATE = 0
COL_D1 = GATE_WIDTH
COL_SWA = COL_D1 + QKV_W
NAT_W = COL_SWA + QKV_W
PROJ_W = NAT_W + (N_DIL - 1) * QKV_W

TM = 512
QT = 512
NEG = -1e30
VMEM_LIMIT = 56 * 1024 * 1024


def _const_spec(shape):
    nd = len(shape)
    return pl.BlockSpec(shape, lambda *_: (0,) * nd, pipeline_mode=pl.Buffered(1))


def _params(n_axes):
    return pltpu.CompilerParams(dimension_semantics=("parallel",) * n_axes,
                                vmem_limit_bytes=VMEM_LIMIT)


def _rms(x, g):
    ms = jnp.mean(x * x, axis=-1, keepdims=True)
    return x * lax.rsqrt(ms + EPS) * g


def _rope(r, cos, sin):
    a = r[:, :LANES]
    b = r[:, LANES:]
    return jnp.concatenate([a * cos - b * sin, b * cos + a * sin], axis=1)


def _proj_kernel(x_ref, g_ref, w_ref, bg_ref, cos_ref, sin_ref, nat_ref, *rest):
    res_refs = rest[:N_DIL - 1]
    hs_ref, hp_ref = rest[N_DIL - 1:]
    h = _rms(x_ref[...], g_ref[...])
    hn = h.astype(jnp.bfloat16)
    cos = cos_ref[...]
    sin = sin_ref[...]
    for c in range(0, NAT_W, HGW):
        r = jnp.dot(hn, w_ref[:, c:c + HGW], preferred_element_type=jnp.float32)
        if c < COL_D1:
            z = r + bg_ref[:, c:c + HGW]
            out = 1.0 / (1.0 + jnp.exp(-z))
        elif (c - COL_D1) % QKV_W < 2 * GROUP_W:
            out = _rope(r, cos, sin)
        else:
            out = r
        nat_ref[:, c:c + HGW] = out.astype(jnp.bfloat16)

    for j in range(D_MODEL // LANES):
        hs_ref[j] = h[:, j * LANES:(j + 1) * LANES]
    for gi, o_ref in enumerate(res_refs):
        d = DIL_GROUPS[gi + 1][1]
        tmd = TM // d
        for r in range(d):
            for j in range(D_MODEL // LANES):
                hp_ref[r * tmd:(r + 1) * tmd, j * LANES:(j + 1) * LANES] = (
                    hs_ref[j, pl.ds(r, tmd, stride=d), :].astype(jnp.bfloat16))
        cosp = jnp.concatenate([cos_ref[pl.ds(r, tmd, stride=d), :] for r in range(d)], axis=0)
        sinp = jnp.concatenate([sin_ref[pl.ds(r, tmd, stride=d), :] for r in range(d)], axis=0)
        hp = hp_ref[...]
        col0 = NAT_W + gi * QKV_W
        for c in range(0, QKV_W, HGW):
            rr = jnp.dot(hp, w_ref[:, col0 + c:col0 + c + HGW],
                         preferred_element_type=jnp.float32)
            out = (_rope(rr, cosp, sinp) if c < 2 * GROUP_W else rr).astype(jnp.bfloat16)
            for r in range(d):
                o_ref[r, :, c:c + HGW] = out[r * tmd:(r + 1) * tmd]


def _proj(x, g, w, bg, cos4, sin4):
    B, S, _ = x.shape
    tok = lambda w_: pl.BlockSpec((None, TM, w_), lambda b, i: (b, i, 0))
    out_specs = [tok(NAT_W)]
    out_shape = [jax.ShapeDtypeStruct((B, S, NAT_W), jnp.bfloat16)]
    for _, d in DIL_GROUPS[1:]:
        out_specs.append(pl.BlockSpec((None, d, TM // d, QKV_W), lambda b, i: (b, 0, i, 0)))
        out_shape.append(jax.ShapeDtypeStruct((B, d, S // d, QKV_W), jnp.bfloat16))
    return pl.pallas_call(
        _proj_kernel,
        name="proj",
        grid=(B, S // TM),
        in_specs=[tok(D_MODEL), _const_spec((1, D_MODEL)), _const_spec((D_MODEL, PROJ_W)),
                  _const_spec((1, GATE_WIDTH)), tok(LANES), tok(LANES)],
        out_specs=out_specs,
        out_shape=out_shape,
        scratch_shapes=[pltpu.VMEM((D_MODEL // LANES, TM, LANES), jnp.float32),
                        pltpu.VMEM((TM, D_MODEL), jnp.bfloat16)],
        compiler_params=_params(2),
    )(x, g, w, bg, cos4, sin4)


def _attn_kernel(sink_ref, q_ref, kc_ref, kp_ref, vc_ref, vp_ref, o_ref, *rest,
                 has_sink, write_lse):
    lse_ref = rest[0] if write_lse else None
    first = pl.program_id(1) == 0
    nqb = QT // BLOCK

    row = lax.broadcasted_iota(jnp.int32, (BLOCK, 2 * BLOCK), 0)
    col = lax.broadcasted_iota(jnp.int32, (BLOCK, 2 * BLOCK), 1)
    band = (col >= row) & (col <= row + BLOCK)
    bias = jnp.where(band, 0.0, NEG).astype(jnp.float32)
    bias_first = jnp.where(band & ((col >= BLOCK) | jnp.logical_not(first)), 0.0, NEG
                           ).astype(jnp.float32)

    lane = lax.broadcasted_iota(jnp.int32, (1, HGW), 1)
    qmask = [jnp.where((lane % LANES) // HALF == j, 1.0, 0.0).astype(jnp.bfloat16)
             for j in range(HG)]
    vhead = lane // HEAD_DIM
    lse_lane = lax.broadcasted_iota(jnp.int32, (BLOCK, LANES), 1)

    lse_tiles = [jnp.zeros((BLOCK, LANES), jnp.float32) for _ in range(nqb)]
    for g in range(2):
        cs = slice(g * HGW, (g + 1) * HGW)
        kall = jnp.concatenate([kp_ref[:, cs], kc_ref[:, cs]], axis=0)
        vall = jnp.concatenate([vp_ref[:, cs], vc_ref[:, cs]], axis=0)
        for qb in range(nqb):
            q4 = q_ref[qb * BLOCK:(qb + 1) * BLOCK, cs]
            kcat = kall[qb * BLOCK:(qb + 2) * BLOCK]
            vcat = vall[qb * BLOCK:(qb + 2) * BLOCK]
            bias_q = bias_first if qb == 0 else bias
            qs = jnp.concatenate([q4 * qmask[j] for j in range(HG)], axis=0)
            s_all = lax.dot_general(qs, kcat, (((1,), (1,)), ((), ())),
                                    preferred_element_type=jnp.float32)
            ps = []
            for j in range(HG):
                s = s_all[j * BLOCK:(j + 1) * BLOCK] + bias_q
                m = jnp.max(s, axis=-1, keepdims=True)
                if has_sink:
                    sk = sink_ref[g * HG + j]
                    m = jnp.maximum(m, sk)
                p = jnp.exp(s - m)
                den = jnp.sum(p, axis=-1, keepdims=True)
                if has_sink:
                    den = den + jnp.exp(sk - m)
                ps.append((p * (1.0 / den)).astype(jnp.bfloat16))
                if write_lse:
                    lse = m + jnp.log(den)
                    lse_tiles[qb] = jnp.where(lse_lane == g * HG + j, lse, lse_tiles[qb])
            o_s = jnp.dot(jnp.concatenate(ps, axis=0), vcat,
                          preferred_element_type=jnp.float32)
            o4 = o_s[:BLOCK]
            for j in range(1, HG):
                o4 = jnp.where(vhead == j, o_s[j * BLOCK:(j + 1) * BLOCK], o4)
            o_ref[qb * BLOCK:(qb + 1) * BLOCK, cs] = o4.astype(jnp.bfloat16)
    if write_lse:
        for qb in range(nqb):
            lse_ref[qb * BLOCK:(qb + 1) * BLOCK, :] = lse_tiles[qb]


def _attn(qkv, cb0, sink, *, has_sink, write_lse):
    N, L, _ = qkv.shape
    r = QT // BLOCK
    cur = lambda cb: pl.BlockSpec((None, QT, GROUP_W), lambda n, i: (n, i, cb))
    prev = lambda cb: pl.BlockSpec((None, BLOCK, GROUP_W),
                                   lambda n, i: (n, jnp.maximum(i * r - 1, 0), cb))
    out_specs = [pl.BlockSpec((None, QT, GROUP_W), lambda n, i: (n, i, 0))]
    out_shape = [jax.ShapeDtypeStruct((N, L, GROUP_W), jnp.bfloat16)]
    if write_lse:
        out_specs.append(pl.BlockSpec((None, QT, LANES), lambda n, i: (n, i, 0)))
        out_shape.append(jax.ShapeDtypeStruct((N, L, LANES), jnp.float32))
    res = pl.pallas_call(
        functools.partial(_attn_kernel, has_sink=has_sink, write_lse=write_lse),
        name="attn_sink" if has_sink else "attn",
        grid=(N, L // QT),
        in_specs=[pl.BlockSpec(memory_space=pltpu.SMEM),
                  cur(cb0), cur(cb0 + 1), prev(cb0 + 1), cur(cb0 + 2), prev(cb0 + 2)],
        out_specs=out_specs,
        out_shape=out_shape,
        compiler_params=_params(2),
    )(sink, qkv, qkv, qkv, qkv, qkv)
    return res if write_lse else res[0]


def _expand_matrix():
    e = np.zeros((2 * LANES, N_DIL * DIL_OUT), np.float32)
    for g in range(N_DIL):
        for hh in range(DIL_HEADS):
            c0 = g * DIL_OUT + hh * HEAD_DIM
            e[g * DIL_HEADS + hh, c0:c0 + HEAD_DIM] = 1.0
            e[LANES + g * DIL_HEADS + hh, c0:c0 + HEAD_DIM] = 1.0
    return jnp.asarray(e, jnp.bfloat16)


def _merge_kernel(x_ref, o0_ref, o1_ref, o2_ref, l0_ref, l1_ref, l2_ref, ob_ref, gt_ref,
                  e_ref, wa_ref, wb_ref, wo_ref, out_ref, onat_ref, lnat_ref):
    res = ((DIL_GROUPS[1][1], o1_ref, l1_ref), (DIL_GROUPS[2][1], o2_ref, l2_ref))
    ls = [l0_ref[...]]
    for gi, (d, _, l_ref) in enumerate(res):
        for r in range(d):
            lnat_ref[gi, pl.ds(r, TM // d, stride=d), :] = l_ref[r]
        ls.append(lnat_ref[gi])
    mx = jnp.maximum(jnp.maximum(ls[0], ls[1]), ls[2])
    es = [jnp.exp(l - mx) for l in ls]
    inv = 1.0 / (es[0] + es[1] + es[2])
    lane = lax.broadcasted_iota(jnp.int32, (1, LANES), 1)
    a = jnp.zeros((TM, LANES), jnp.float32)
    for gi, e in enumerate(es):
        ag = jnp.where(lane < DIL_HEADS, e * inv, 0.0)
        a = a + (pltpu.roll(ag, gi * DIL_HEADS, 1) if gi else ag)
    a_hi = a.astype(jnp.bfloat16)
    a_lo = (a - a_hi.astype(jnp.float32)).astype(jnp.bfloat16)
    aa = jnp.concatenate([a_hi, a_lo], axis=1)
    oa = jnp.dot(aa, e_ref[:, :DIL_OUT], preferred_element_type=jnp.float32
                 ) * o0_ref[...].astype(jnp.float32)
    for gi, (d, o_ref, _) in enumerate(res):
        for r in range(d):
            of = o_ref[r].astype(jnp.float32)
            for s in range(DIL_OUT // LANES):
                onat_ref[s, pl.ds(r, TM // d, stride=d), :] = of[:, s * LANES:(s + 1) * LANES]
        og = jnp.concatenate([onat_ref[s] for s in range(DIL_OUT // LANES)], axis=1)
        wide = jnp.dot(aa, e_ref[:, (gi + 1) * DIL_OUT:(gi + 2) * DIL_OUT],
                       preferred_element_type=jnp.float32)
        oa = oa + wide * og
    ya = jnp.dot(oa.astype(jnp.bfloat16), wa_ref[...], preferred_element_type=jnp.float32)
    yb = jnp.dot(ob_ref[...], wb_ref[...], preferred_element_type=jnp.float32)
    merged = (gt_ref[:, :D_MODEL].astype(jnp.float32) * ya
              + gt_ref[:, D_MODEL:].astype(jnp.float32) * yb)
    out_ref[...] = x_ref[...] + jnp.dot(merged.astype(jnp.bfloat16), wo_ref[...],
                                        preferred_element_type=jnp.float32)


def _merge(x, nat, o_dil, lse_dil, o_swa, wa, wb, wo):
    B, S, _ = x.shape
    tok = lambda w, cb=0: pl.BlockSpec((None, TM, w), lambda b, i: (b, i, cb))
    res = lambda d, w: pl.BlockSpec((None, d, TM // d, w), lambda b, i: (b, 0, i, 0))
    d1, d2 = DIL_GROUPS[1][1], DIL_GROUPS[2][1]
    return pl.pallas_call(
        _merge_kernel,
        name="merge",
        grid=(B, S // TM),
        in_specs=[tok(D_MODEL), tok(GROUP_W), res(d1, GROUP_W), res(d2, GROUP_W),
                  tok(LANES), res(d1, LANES), res(d2, LANES),
                  tok(GROUP_W), tok(GATE_WIDTH, COL_GATE // GATE_WIDTH),
                  _const_spec((2 * LANES, N_DIL * DIL_OUT)),
                  _const_spec((DIL_OUT, D_MODEL)), _const_spec((SWA_Q_WIDTH, D_MODEL)),
                  _const_spec((D_MODEL, D_MODEL))],
        out_specs=tok(D_MODEL),
        out_shape=jax.ShapeDtypeStruct((B, S, D_MODEL), jnp.float32),
        scratch_shapes=[pltpu.VMEM((DIL_OUT // LANES, TM, LANES), jnp.float32),
                        pltpu.VMEM((N_DIL - 1, TM, LANES), jnp.float32)],
        compiler_params=_params(2),
    )(x, o_dil[0], o_dil[1].reshape(B, d1, S // d1, GROUP_W),
      o_dil[2].reshape(B, d2, S // d2, GROUP_W),
      lse_dil[0], lse_dil[1].reshape(B, d1, S // d1, LANES),
      lse_dil[2].reshape(B, d2, S // d2, LANES),
      o_swa, nat, _expand_matrix(), wa, wb, wo)


def _memkv_kernel(mem_ref, g_ref, w_ref, kv_ref):
    mn = _rms(mem_ref[...], g_ref[...]).astype(jnp.bfloat16)
    kv_ref[...] = jnp.dot(mn, w_ref[...], preferred_element_type=jnp.float32
                          ).astype(jnp.bfloat16)


def _memkv(mem, g, w):
    B, M, _ = mem.shape
    return pl.pallas_call(
        _memkv_kernel,
        name="memkv",
        grid=(B,),
        in_specs=[pl.BlockSpec((None, M, D_MODEL), lambda b: (b, 0, 0)),
                  _const_spec((1, D_MODEL)), _const_spec((D_MODEL, 2 * D_MODEL))],
        out_specs=pl.BlockSpec((None, M, 2 * D_MODEL), lambda b: (b, 0, 0)),
        out_shape=jax.ShapeDtypeStruct((B, M, 2 * D_MODEL), jnp.bfloat16),
        compiler_params=_params(1),
    )(mem, g, w)


def _cross_kernel(x_ref, g_ref, wq_ref, kv_ref, wo_ref, out_ref):
    x = x_ref[...]
    hc = _rms(x, g_ref[...]).astype(jnp.bfloat16)
    q = jnp.dot(hc, wq_ref[...], preferred_element_type=jnp.float32).astype(jnp.bfloat16)
    os = []
    for hh in range(X_HEADS):
        cs = slice(hh * X_HEAD_DIM, (hh + 1) * X_HEAD_DIM)
        s = lax.dot_general(q[:, cs], kv_ref[:, cs], (((1,), (1,)), ((), ())),
                            preferred_element_type=jnp.float32)
        m = jnp.max(s, axis=-1, keepdims=True)
        p = jnp.exp(s - m)
        p = p * (1.0 / jnp.sum(p, axis=-1, keepdims=True))
        vs = slice(D_MODEL + hh * X_HEAD_DIM, D_MODEL + (hh + 1) * X_HEAD_DIM)
        os.append(jnp.dot(p.astype(jnp.bfloat16), kv_ref[:, vs],
                          preferred_element_type=jnp.float32).astype(jnp.bfloat16))
    o = jnp.concatenate(os, axis=1)
    out_ref[...] = x + jnp.dot(o, wo_ref[...], preferred_element_type=jnp.float32)


def _cross(x, g, wq, kv, wo):
    B, S, _ = x.shape
    M = kv.shape[1]
    tok = pl.BlockSpec((None, TM, D_MODEL), lambda b, i: (b, i, 0))
    return pl.pallas_call(
        _cross_kernel,
        name="cross",
        grid=(B, S // TM),
        in_specs=[tok, _const_spec((1, D_MODEL)), _const_spec((D_MODEL, D_MODEL)),
                  pl.BlockSpec((None, M, 2 * D_MODEL), lambda b, i: (b, 0, 0)),
                  _const_spec((D_MODEL, D_MODEL))],
        out_specs=tok,
        out_shape=jax.ShapeDtypeStruct((B, S, D_MODEL), jnp.float32),
        compiler_params=_params(2),
    )(x, g, wq, kv, wo)


FF_CHUNK = 1024


def _mlp_kernel(x_ref, g_ref, w1_ref, w2_ref, gf_ref, out_ref, *, final_norm):
    x = x_ref[...]
    h = _rms(x, g_ref[...]).astype(jnp.bfloat16)
    y = x
    for c in range(0, D_FF, FF_CHUNK):
        a = jnp.maximum(jnp.dot(h, w1_ref[:, c:c + FF_CHUNK],
                                preferred_element_type=jnp.float32), 0.0)
        y = y + jnp.dot((a * a).astype(jnp.bfloat16), w2_ref[c:c + FF_CHUNK, :],
                        preferred_element_type=jnp.float32)
    out_ref[...] = _rms(y, gf_ref[...]) if final_norm else y


def _mlp(x, g, w1, w2, gf, final_norm):
    B, S, _ = x.shape
    tok = pl.BlockSpec((None, TM, D_MODEL), lambda b, i: (b, i, 0))
    return pl.pallas_call(
        functools.partial(_mlp_kernel, final_norm=final_norm),
        name="mlp",
        grid=(B, S // TM),
        in_specs=[tok, _const_spec((1, D_MODEL)), _const_spec((D_MODEL, D_FF)),
                  _const_spec((D_FF, D_MODEL)), _const_spec((1, D_MODEL))],
        out_specs=tok,
        out_shape=jax.ShapeDtypeStruct((B, S, D_MODEL), jnp.float32),
        compiler_params=_params(2),
    )(x, g, w1, w2, gf)


def _split_half(w, n_heads):
    k = w.shape[0]
    w = w.reshape(k, n_heads // HG, HG, 2, HALF)
    return w.transpose(0, 1, 3, 2, 4).reshape(k, n_heads * HEAD_DIM)


def _permute_w_in(w_in):
    scale = HEAD_DIM ** -0.5
    c = 0
    qa = w_in[:, c:c + DIL_WIDTH]; c += DIL_WIDTH
    ka = w_in[:, c:c + DIL_WIDTH]; c += DIL_WIDTH
    va = w_in[:, c:c + DIL_WIDTH]; c += DIL_WIDTH
    qb = w_in[:, c:c + SWA_Q_WIDTH]; c += SWA_Q_WIDTH
    kb = w_in[:, c:c + SWA_KV_WIDTH]; c += SWA_KV_WIDTH
    vb = w_in[:, c:c + SWA_KV_WIDTH]; c += SWA_KV_WIDTH
    gt = w_in[:, c:]
    kd = w_in.shape[0]

    def dil(g):
        sl = slice(g * GROUP_W, (g + 1) * GROUP_W)
        return [_split_half(qa[:, sl] * scale, DIL_HEADS), _split_half(ka[:, sl], DIL_HEADS),
                va[:, sl]]

    kb = kb.reshape(kd, SWA_KV_HEADS, 2, 1, HALF)
    kb = jnp.broadcast_to(kb, (kd, SWA_KV_HEADS, 2, HG, HALF)).reshape(kd, GROUP_W)
    vb = vb.reshape(kd, SWA_KV_HEADS, 1, HEAD_DIM)
    vb = jnp.broadcast_to(vb, (kd, SWA_KV_HEADS, HG, HEAD_DIM)).reshape(kd, GROUP_W)
    cols = [gt] + dil(0) + [_split_half(qb * scale, SWA_Q_HEADS), kb, vb]
    for g in range(1, N_DIL):
        cols += dil(g)
    return jnp.concatenate(cols, axis=1).astype(jnp.bfloat16)


def kernel(x, mem, positions, g_mix, w_in, b_gate, sink, w_branch_a, w_branch_b, w_out,
           g_cross, g_mem, w_cq, w_ckv, w_co, g_mlp, w_1, w_2, g_final):
    bf = jnp.bfloat16
    B, S, _ = x.shape
    depth = g_mix.shape[0]
    inv = ROPE_THETA ** (-jnp.arange(HALF, dtype=jnp.float32) / HALF)
    ang = positions.astype(jnp.float32)[..., None] * inv
    cos4 = jnp.tile(jnp.cos(ang), (1, 1, HG))
    sin4 = jnp.tile(jnp.sin(ang), (1, 1, HG))
    for l in range(depth):
        nat, *res = _proj(x, g_mix[l][None], _permute_w_in(w_in[l]),
                          b_gate[l].reshape(1, GATE_WIDTH), cos4, sin4)
        o_dil, lse_dil = [], []
        for gi, (_, d) in enumerate(DIL_GROUPS):
            if gi == 0:
                o, lse = _attn(nat, COL_D1 // GROUP_W, sink[l], has_sink=False, write_lse=True)
            else:
                o, lse = _attn(res[gi - 1].reshape(B * d, S // d, QKV_W), 0, sink[l],
                               has_sink=False, write_lse=True)
            o_dil.append(o)
            lse_dil.append(lse)
        o_swa = _attn(nat, COL_SWA // GROUP_W, sink[l], has_sink=True, write_lse=False)
        x = _merge(x, nat, o_dil, lse_dil, o_swa, w_branch_a[l].astype(bf),
                   w_branch_b[l].astype(bf), w_out[l].astype(bf))
        kv = _memkv(mem, g_mem[l][None], w_ckv[l].astype(bf))
        x = _cross(x, g_cross[l][None], (w_cq[l] * (X_HEAD_DIM ** -0.5)).astype(bf), kv,
                   w_co[l].astype(bf))
        x = _mlp(x, g_mlp[l][None], w_1[l].astype(bf), w_2[l].astype(bf), g_final[None],
                 final_norm=(l == depth - 1))
    return x
```

```python
import functools

import numpy as np
import jax
import jax.numpy as jnp
from jax import lax
from jax.experimental import pallas as pl
from jax.experimental.pallas import tpu as pltpu

D_MODEL = 1024
HEAD_DIM = 64
HALF = HEAD_DIM // 2
DIL_GROUPS = ((128, 1), (512, 4), (2048, 16))
N_DIL = len(DIL_GROUPS)
DIL_HEADS = 8
SWA_Q_HEADS = 8
SWA_KV_HEADS = 2
BLOCK = 128
ROPE_THETA = 10000.0
X_HEADS = 4
X_HEAD_DIM = D_MODEL // X_HEADS
D_FF = 4 * D_MODEL
EPS = 1e-6
N_BRANCH = 2

DIL_WIDTH = N_DIL * DIL_HEADS * HEAD_DIM
DIL_OUT = DIL_HEADS * HEAD_DIM
SWA_Q_WIDTH = SWA_Q_HEADS * HEAD_DIM
SWA_KV_WIDTH = SWA_KV_HEADS * HEAD_DIM
GATE_WIDTH = N_BRANCH * D_MODEL

LANES = 128
HG = 4
HGW = HG * HEAD_DIM
GROUP_W = 2 * HGW
QKV_W = 3 * GROUP_W

COL_GATE = 0
COL_D1 = GATE_WIDTH
COL_SWA = COL_D1 + QKV_W
NAT_W = COL_SWA + QKV_W
PROJ_W = NAT_W + (N_DIL - 1) * QKV_W

TM = 512
TP = 1024
SUB = 512
QT = 2048
NEG = -1e30
VMEM_LIMIT = 56 * 1024 * 1024


def _const_spec(shape):
    nd = len(shape)
    return pl.BlockSpec(shape, lambda *_: (0,) * nd, pipeline_mode=pl.Buffered(1))


def _params(n_axes):
    return pltpu.CompilerParams(dimension_semantics=("parallel",) * n_axes,
                                vmem_limit_bytes=VMEM_LIMIT)


def _rms(x, g):
    ms = jnp.mean(x * x, axis=-1, keepdims=True)
    return x * lax.rsqrt(ms + EPS) * g


def _rope(r, cos, sin):
    a = r[:, :LANES]
    b = r[:, LANES:]
    return jnp.concatenate([a * cos - b * sin, b * cos + a * sin], axis=1)


def _proj_kernel(x_ref, g_ref, w_ref, bg_ref, pos_ref, inv_ref, nat_ref, *rest):
    res_refs = rest[:N_DIL - 1]
    hs_ref, hp_ref, cos_ref, sin_ref = rest[N_DIL - 1:]
    h = _rms(x_ref[...], g_ref[...])
    hn = h.astype(jnp.bfloat16)
    ang = pos_ref[...] * inv_ref[...]
    lane_grp = lax.broadcasted_iota(jnp.int32, (1, LANES), 1) // HALF
    for trig, t_ref in ((jnp.cos, cos_ref), (jnp.sin, sin_ref)):
        t = trig(ang)
        rolled = [t] + [pltpu.roll(t, HALF * k, 1) for k in range(1, HG)]
        for q in range(HG):
            rep = rolled[(0 - q) % HG]
            for k in range(1, HG):
                rep = jnp.where(lane_grp == k, rolled[(k - q) % HG], rep)
            t_ref[pl.ds(q, TM // HG, stride=HG), :] = rep
    cos = cos_ref[...]
    sin = sin_ref[...]

    for j in range(D_MODEL // LANES):
        hs_ref[j] = h[:, j * LANES:(j + 1) * LANES]
    for gi in range(N_DIL - 1):
        d = DIL_GROUPS[gi + 1][1]
        tmd = TM // d
        for r in range(d):
            for j in range(D_MODEL // LANES):
                hp_ref[gi, r * tmd:(r + 1) * tmd, j * LANES:(j + 1) * LANES] = (
                    hs_ref[j, pl.ds(r, tmd, stride=d), :].astype(jnp.bfloat16))

    for c in range(0, NAT_W, HGW):
        r = jnp.dot(hn, w_ref[:, c:c + HGW], preferred_element_type=jnp.float32)
        if c < COL_D1:
            z = r + bg_ref[:, c:c + HGW]
            out = 1.0 / (1.0 + jnp.exp(-z))
        elif (c - COL_D1) % QKV_W < 2 * GROUP_W:
            out = _rope(r, cos, sin)
        else:
            out = r
        nat_ref[:, c:c + HGW] = out.astype(jnp.bfloat16)

    for gi, o_ref in enumerate(res_refs):
        d = DIL_GROUPS[gi + 1][1]
        tmd = TM // d
        cosp = jnp.concatenate([cos_ref[pl.ds(r, tmd, stride=d), :] for r in range(d)], axis=0)
        sinp = jnp.concatenate([sin_ref[pl.ds(r, tmd, stride=d), :] for r in range(d)], axis=0)
        hp = hp_ref[gi]
        col0 = NAT_W + gi * QKV_W
        for c in range(0, QKV_W, HGW):
            rr = jnp.dot(hp, w_ref[:, col0 + c:col0 + c + HGW],
                         preferred_element_type=jnp.float32)
            out = (_rope(rr, cosp, sinp) if c < 2 * GROUP_W else rr).astype(jnp.bfloat16)
            for r in range(d):
                o_ref[r, :, c:c + HGW] = out[r * tmd:(r + 1) * tmd]


def _proj(x, g, w, bg, pos, inv4):
    B, S, _ = x.shape
    tok = lambda w_: pl.BlockSpec((None, TM, w_), lambda b, i: (b, i, 0))
    out_specs = [tok(NAT_W)]
    out_shape = [jax.ShapeDtypeStruct((B, S, NAT_W), jnp.bfloat16)]
    for _, d in DIL_GROUPS[1:]:
        out_specs.append(pl.BlockSpec((None, d, TM // d, QKV_W), lambda b, i: (b, 0, i, 0)))
        out_shape.append(jax.ShapeDtypeStruct((B, d, S // d, QKV_W), jnp.bfloat16))
    return pl.pallas_call(
        _proj_kernel,
        name="proj",
        grid=(B, S // TM),
        in_specs=[tok(D_MODEL), _const_spec((1, D_MODEL)), _const_spec((D_MODEL, PROJ_W)),
                  _const_spec((1, GATE_WIDTH)),
                  pl.BlockSpec((None, TM // HG, LANES), lambda b, i: (b, i, 0)),
                  _const_spec((1, LANES))],
        out_specs=out_specs,
        out_shape=out_shape,
        scratch_shapes=[pltpu.VMEM((D_MODEL // LANES, TM, LANES), jnp.float32),
                        pltpu.VMEM((N_DIL - 1, TM, D_MODEL), jnp.bfloat16),
                        pltpu.VMEM((TM, LANES), jnp.float32),
                        pltpu.VMEM((TM, LANES), jnp.float32)],
        compiler_params=_params(2),
    )(x, g, w, bg, pos, inv4)


def _attn_kernel(sink_ref, q_ref, kc_ref, kp_ref, vc_ref, vp_ref, o_ref, *rest,
                 has_sink, write_lse):
    lse_ref = rest[0] if write_lse else None
    first = pl.program_id(1) == 0
    nseq, qt, _ = q_ref.shape

    row = lax.broadcasted_iota(jnp.int32, (BLOCK, 2 * BLOCK), 0)
    col = lax.broadcasted_iota(jnp.int32, (BLOCK, 2 * BLOCK), 1)
    band = (col >= row) & (col <= row + BLOCK)
    bias = jnp.where(band, 0.0, NEG).astype(jnp.float32)
    bias_first = jnp.where(band & ((col >= BLOCK) | jnp.logical_not(first)), 0.0, NEG
                           ).astype(jnp.float32)

    lane = lax.broadcasted_iota(jnp.int32, (1, HGW), 1)
    qmask = [jnp.where((lane % LANES) // HALF == j, 1.0, 0.0).astype(jnp.bfloat16)
             for j in range(HG)]
    vhead = lane // HEAD_DIM
    lse_lane = lax.broadcasted_iota(jnp.int32, (BLOCK, LANES), 1)

    for n in range(nseq):
        for qb in range(qt // BLOCK):
            rows = slice(qb * BLOCK, (qb + 1) * BLOCK)
            lse_tile = jnp.zeros((BLOCK, LANES), jnp.float32)
            for g in range(2):
                cs = slice(g * HGW, (g + 1) * HGW)
                if qb == 0:
                    kcat = jnp.concatenate([kp_ref[n, :, cs], kc_ref[n, rows, cs]], axis=0)
                    vcat = jnp.concatenate([vp_ref[n, :, cs], vc_ref[n, rows, cs]], axis=0)
                else:
                    kcat = kc_ref[n, (qb - 1) * BLOCK:(qb + 1) * BLOCK, cs]
                    vcat = vc_ref[n, (qb - 1) * BLOCK:(qb + 1) * BLOCK, cs]
                bias_q = bias_first if qb == 0 else bias
                q4 = q_ref[n, rows, cs]
                qs = jnp.concatenate([q4 * qmask[j] for j in range(HG)], axis=0)
                s_all = lax.dot_general(qs, kcat, (((1,), (1,)), ((), ())),
                                        preferred_element_type=jnp.float32)
                ps = []
                for j in range(HG):
                    s = s_all[j * BLOCK:(j + 1) * BLOCK] + bias_q
                    m = jnp.max(s, axis=-1, keepdims=True)
                    if has_sink:
                        sk = sink_ref[g * HG + j]
                        m = jnp.maximum(m, sk)
                    p = jnp.exp(s - m)
                    den = jnp.sum(p, axis=-1, keepdims=True)
                    if has_sink:
                        den = den + jnp.exp(sk - m)
                    ps.append((p * (1.0 / den)).astype(jnp.bfloat16))
                    if write_lse:
                        lse = m + jnp.log(den)
                        lse_tile = jnp.where(lse_lane == g * HG + j, lse, lse_tile)
                o_s = jnp.dot(jnp.concatenate(ps, axis=0), vcat,
                              preferred_element_type=jnp.float32)
                o4 = o_s[:BLOCK]
                for j in range(1, HG):
                    o4 = jnp.where(vhead == j, o_s[j * BLOCK:(j + 1) * BLOCK], o4)
                o_ref[n, rows, cs] = o4.astype(jnp.bfloat16)
            if write_lse:
                lse_ref[n, rows, :] = lse_tile


def _attn(qkv, cb0, sink, *, has_sink, write_lse):
    N, L, _ = qkv.shape
    qt = min(QT, L)
    ns = QT // qt
    r = qt // BLOCK
    cur = lambda cb: pl.BlockSpec((ns, qt, GROUP_W), lambda n, i: (n, i, cb))
    prev = lambda cb: pl.BlockSpec((ns, BLOCK, GROUP_W),
                                   lambda n, i: (n, jnp.maximum(i * r - 1, 0), cb))
    out_specs = [pl.BlockSpec((ns, qt, GROUP_W), lambda n, i: (n, i, 0))]
    out_shape = [jax.ShapeDtypeStruct((N, L, GROUP_W), jnp.bfloat16)]
    if write_lse:
        out_specs.append(pl.BlockSpec((ns, qt, LANES), lambda n, i: (n, i, 0)))
        out_shape.append(jax.ShapeDtypeStruct((N, L, LANES), jnp.float32))
    res = pl.pallas_call(
        functools.partial(_attn_kernel, has_sink=has_sink, write_lse=write_lse),
        name="attn_sink" if has_sink else "attn",
        grid=(N // ns, L // qt),
        in_specs=[pl.BlockSpec(memory_space=pltpu.SMEM),
                  cur(cb0), cur(cb0 + 1), prev(cb0 + 1), cur(cb0 + 2), prev(cb0 + 2)],
        out_specs=out_specs,
        out_shape=out_shape,
        compiler_params=_params(2),
    )(sink, qkv, qkv, qkv, qkv, qkv)
    return res if write_lse else res[0]


def _expand_matrix():
    e = np.zeros((2 * LANES, N_DIL * DIL_OUT), np.float32)
    for g in range(N_DIL):
        for hh in range(DIL_HEADS):
            c0 = g * DIL_OUT + hh * HEAD_DIM
            e[g * DIL_HEADS + hh, c0:c0 + HEAD_DIM] = 1.0
            e[LANES + g * DIL_HEADS + hh, c0:c0 + HEAD_DIM] = 1.0
    return jnp.asarray(e, jnp.bfloat16)


def _merge_kernel(x_ref, o0_ref, o1_ref, o2_ref, l0_ref, l1_ref, l2_ref, ob_ref, gt_ref,
                  e_ref, wa_ref, wb_ref, wo_ref, out_ref, onat_ref, lnat_ref):
    res = ((DIL_GROUPS[1][1], o1_ref, l1_ref), (DIL_GROUPS[2][1], o2_ref, l2_ref))
    ls = [l0_ref[...]]
    for gi, (d, _, l_ref) in enumerate(res):
        for r in range(d):
            lnat_ref[gi, pl.ds(r, TP // d, stride=d), :] = l_ref[r]
        ls.append(lnat_ref[gi])
    mx = jnp.maximum(jnp.maximum(ls[0], ls[1]), ls[2])
    es = [jnp.exp(l - mx) for l in ls]
    inv = 1.0 / (es[0] + es[1] + es[2])
    lane = lax.broadcasted_iota(jnp.int32, (1, LANES), 1)
    a = jnp.zeros((TP, LANES), jnp.float32)
    for gi, e in enumerate(es):
        ag = jnp.where(lane < DIL_HEADS, e * inv, 0.0)
        a = a + (pltpu.roll(ag, gi * DIL_HEADS, 1) if gi else ag)
    a_hi = a.astype(jnp.bfloat16)
    a_lo = (a - a_hi.astype(jnp.float32)).astype(jnp.bfloat16)
    aa = jnp.concatenate([a_hi, a_lo], axis=1)
    oa = jnp.dot(aa, e_ref[:, :DIL_OUT], preferred_element_type=jnp.float32
                 ) * o0_ref[...].astype(jnp.float32)
    for gi, (d, o_ref, _) in enumerate(res):
        for r in range(d):
            of = o_ref[r].astype(jnp.float32)
            for s in range(DIL_OUT // LANES):
                onat_ref[s, pl.ds(r, TP // d, stride=d), :] = of[:, s * LANES:(s + 1) * LANES]
        og = jnp.concatenate([onat_ref[s] for s in range(DIL_OUT // LANES)], axis=1)
        wide = jnp.dot(aa, e_ref[:, (gi + 1) * DIL_OUT:(gi + 2) * DIL_OUT],
                       preferred_element_type=jnp.float32)
        oa = oa + wide * og
    ya = jnp.dot(oa.astype(jnp.bfloat16), wa_ref[...], preferred_element_type=jnp.float32)
    yb = jnp.dot(ob_ref[...], wb_ref[...], preferred_element_type=jnp.float32)
    merged = (gt_ref[:, :D_MODEL].astype(jnp.float32) * ya
              + gt_ref[:, D_MODEL:].astype(jnp.float32) * yb)
    out_ref[...] = x_ref[...] + jnp.dot(merged.astype(jnp.bfloat16), wo_ref[...],
                                        preferred_element_type=jnp.float32)


def _merge(x, nat, o_dil, lse_dil, o_swa, wa, wb, wo):
    B, S, _ = x.shape
    tok = lambda w, cb=0: pl.BlockSpec((None, TP, w), lambda b, i: (b, i, cb))
    res = lambda d, w: pl.BlockSpec((None, d, TP // d, w), lambda b, i: (b, 0, i, 0))
    d1, d2 = DIL_GROUPS[1][1], DIL_GROUPS[2][1]
    return pl.pallas_call(
        _merge_kernel,
        name="merge",
        grid=(B, S // TP),
        in_specs=[tok(D_MODEL), tok(GROUP_W), res(d1, GROUP_W), res(d2, GROUP_W),
                  tok(LANES), res(d1, LANES), res(d2, LANES),
                  tok(GROUP_W), tok(GATE_WIDTH, COL_GATE // GATE_WIDTH),
                  _const_spec((2 * LANES, N_DIL * DIL_OUT)),
                  _const_spec((DIL_OUT, D_MODEL)), _const_spec((SWA_Q_WIDTH, D_MODEL)),
                  _const_spec((D_MODEL, D_MODEL))],
        out_specs=tok(D_MODEL),
        out_shape=jax.ShapeDtypeStruct((B, S, D_MODEL), jnp.float32),
        scratch_shapes=[pltpu.VMEM((DIL_OUT // LANES, TP, LANES), jnp.float32),
                        pltpu.VMEM((N_DIL - 1, TP, LANES), jnp.float32)],
        compiler_params=_params(2),
    )(x, o_dil[0], o_dil[1].reshape(B, d1, S // d1, GROUP_W),
      o_dil[2].reshape(B, d2, S // d2, GROUP_W),
      lse_dil[0], lse_dil[1].reshape(B, d1, S // d1, LANES),
      lse_dil[2].reshape(B, d2, S // d2, LANES),
      o_swa, nat, _expand_matrix(), wa, wb, wo)


def _memkv_kernel(mem_ref, g_ref, w_ref, kv_ref):
    mn = _rms(mem_ref[...], g_ref[...]).astype(jnp.bfloat16)
    kv_ref[...] = jnp.dot(mn, w_ref[...], preferred_element_type=jnp.float32
                          ).astype(jnp.bfloat16)


def _memkv(mem, g, w):
    B, M, _ = mem.shape
    return pl.pallas_call(
        _memkv_kernel,
        name="memkv",
        grid=(B,),
        in_specs=[pl.BlockSpec((None, M, D_MODEL), lambda b: (b, 0, 0)),
                  _const_spec((1, D_MODEL)), _const_spec((D_MODEL, 2 * D_MODEL))],
        out_specs=pl.BlockSpec((None, M, 2 * D_MODEL), lambda b: (b, 0, 0)),
        out_shape=jax.ShapeDtypeStruct((B, M, 2 * D_MODEL), jnp.bfloat16),
        compiler_params=_params(1),
    )(mem, g, w)


def _cross_kernel(x_ref, g_ref, wq_ref, kv_ref, wo_ref, out_ref):
    for r0 in range(0, TP, SUB):
        rows = slice(r0, r0 + SUB)
        x = x_ref[rows, :]
        hc = _rms(x, g_ref[...]).astype(jnp.bfloat16)
        q = jnp.dot(hc, wq_ref[...], preferred_element_type=jnp.float32).astype(jnp.bfloat16)
        os = []
        for hh in range(X_HEADS):
            cs = slice(hh * X_HEAD_DIM, (hh + 1) * X_HEAD_DIM)
            s = lax.dot_general(q[:, cs], kv_ref[:, cs], (((1,), (1,)), ((), ())),
                                preferred_element_type=jnp.float32)
            m = jnp.max(s, axis=-1, keepdims=True)
            p = jnp.exp(s - m)
            p = p * (1.0 / jnp.sum(p, axis=-1, keepdims=True))
            vs = slice(D_MODEL + hh * X_HEAD_DIM, D_MODEL + (hh + 1) * X_HEAD_DIM)
            os.append(jnp.dot(p.astype(jnp.bfloat16), kv_ref[:, vs],
                              preferred_element_type=jnp.float32).astype(jnp.bfloat16))
        o = jnp.concatenate(os, axis=1)
        out_ref[rows, :] = x + jnp.dot(o, wo_ref[...], preferred_element_type=jnp.float32)


def _cross(x, g, wq, kv, wo):
    B, S, _ = x.shape
    M = kv.shape[1]
    tok = pl.BlockSpec((None, TP, D_MODEL), lambda b, i: (b, i, 0))
    return pl.pallas_call(
        _cross_kernel,
        name="cross",
        grid=(B, S // TP),
        in_specs=[tok, _const_spec((1, D_MODEL)), _const_spec((D_MODEL, D_MODEL)),
                  pl.BlockSpec((None, M, 2 * D_MODEL), lambda b, i: (b, 0, 0)),
                  _const_spec((D_MODEL, D_MODEL))],
        out_specs=tok,
        out_shape=jax.ShapeDtypeStruct((B, S, D_MODEL), jnp.float32),
        compiler_params=_params(2),
    )(x, g, wq, kv, wo)


FF_CHUNK = 1024


def _mlp_kernel(x_ref, g_ref, w1_ref, w2_ref, gf_ref, out_ref, *, final_norm):
    x = x_ref[...]
    h = _rms(x, g_ref[...]).astype(jnp.bfloat16)
    y = x
    for c in range(0, D_FF, FF_CHUNK):
        a = jnp.maximum(jnp.dot(h, w1_ref[:, c:c + FF_CHUNK],
                                preferred_element_type=jnp.float32), 0.0)
        y = y + jnp.dot((a * a).astype(jnp.bfloat16), w2_ref[c:c + FF_CHUNK, :],
                        preferred_element_type=jnp.float32)
    out_ref[...] = _rms(y, gf_ref[...]) if final_norm else y


def _mlp(x, g, w1, w2, gf, final_norm):
    B, S, _ = x.shape
    tok = pl.BlockSpec((None, TP, D_MODEL), lambda b, i: (b, i, 0))
    return pl.pallas_call(
        functools.partial(_mlp_kernel, final_norm=final_norm),
        name="mlp",
        grid=(B, S // TP),
        in_specs=[tok, _const_spec((1, D_MODEL)), _const_spec((D_MODEL, D_FF)),
                  _const_spec((D_FF, D_MODEL)), _const_spec((1, D_MODEL))],
        out_specs=tok,
        out_shape=jax.ShapeDtypeStruct((B, S, D_MODEL), jnp.float32),
        compiler_params=_params(2),
    )(x, g, w1, w2, gf)


def _split_half(w, n_heads):
    k = w.shape[0]
    w = w.reshape(k, n_heads // HG, HG, 2, HALF)
    return w.transpose(0, 1, 3, 2, 4).reshape(k, n_heads * HEAD_DIM)


def _permute_w_in(w_in):
    scale = HEAD_DIM ** -0.5
    c = 0
    qa = w_in[:, c:c + DIL_WIDTH]; c += DIL_WIDTH
    ka = w_in[:, c:c + DIL_WIDTH]; c += DIL_WIDTH
    va = w_in[:, c:c + DIL_WIDTH]; c += DIL_WIDTH
    qb = w_in[:, c:c + SWA_Q_WIDTH]; c += SWA_Q_WIDTH
    kb = w_in[:, c:c + SWA_KV_WIDTH]; c += SWA_KV_WIDTH
    vb = w_in[:, c:c + SWA_KV_WIDTH]; c += SWA_KV_WIDTH
    gt = w_in[:, c:]
    kd = w_in.shape[0]

    def dil(g):
        sl = slice(g * GROUP_W, (g + 1) * GROUP_W)
        return [_split_half(qa[:, sl] * scale, DIL_HEADS), _split_half(ka[:, sl], DIL_HEADS),
                va[:, sl]]

    kb = kb.reshape(kd, SWA_KV_HEADS, 2, 1, HALF)
    kb = jnp.broadcast_to(kb, (kd, SWA_KV_HEADS, 2, HG, HALF)).reshape(kd, GROUP_W)
    vb = vb.reshape(kd, SWA_KV_HEADS, 1, HEAD_DIM)
    vb = jnp.broadcast_to(vb, (kd, SWA_KV_HEADS, HG, HEAD_DIM)).reshape(kd, GROUP_W)
    cols = [gt] + dil(0) + [_split_half(qb * scale, SWA_Q_HEADS), kb, vb]
    for g in range(1, N_DIL):
        cols += dil(g)
    return jnp.concatenate(cols, axis=1).astype(jnp.bfloat16)


def kernel(x, mem, positions, g_mix, w_in, b_gate, sink, w_branch_a, w_branch_b, w_out,
           g_cross, g_mem, w_cq, w_ckv, w_co, g_mlp, w_1, w_2, g_final):
    bf = jnp.bfloat16
    B, S, _ = x.shape
    depth = g_mix.shape[0]
    inv = ROPE_THETA ** (-jnp.arange(HALF, dtype=jnp.float32) / HALF)
    inv4 = jnp.tile(inv, HG)[None]
    pos = jnp.repeat(positions.astype(jnp.float32).reshape(B, S // HG, HG), HALF, axis=-1)
    for l in range(depth):
        nat, *res = _proj(x, g_mix[l][None], _permute_w_in(w_in[l]),
                          b_gate[l].reshape(1, GATE_WIDTH), pos, inv4)
        o_dil, lse_dil = [], []
        for gi, (_, d) in enumerate(DIL_GROUPS):
            if gi == 0:
                o, lse = _attn(nat, COL_D1 // GROUP_W, sink[l], has_sink=False, write_lse=True)
            else:
                o, lse = _attn(res[gi - 1].reshape(B * d, S // d, QKV_W), 0, sink[l],
                               has_sink=False, write_lse=True)
            o_dil.append(o)
            lse_dil.append(lse)
        o_swa = _attn(nat, COL_SWA // GROUP_W, sink[l], has_sink=True, write_lse=False)
        x = _merge(x, nat, o_dil, lse_dil, o_swa, w_branch_a[l].astype(bf),
                   w_branch_b[l].astype(bf), w_out[l].astype(bf))
        kv = _memkv(mem, g_mem[l][None], w_ckv[l].astype(bf))
        x = _cross(x, g_cross[l][None], (w_cq[l] * (X_HEAD_DIM ** -0.5)).astype(bf), kv,
                   w_co[l].astype(bf))
        x = _mlp(x, g_mlp[l][None], w_1[l].astype(bf), w_2[l].astype(bf), g_final[None],
                 final_norm=(l == depth - 1))
    return x
```

```python
import functools

import numpy as np
import jax
import jax.numpy as jnp
from jax import lax
from jax.experimental import pallas as pl
from jax.experimental.pallas import tpu as pltpu

D_MODEL = 1024
HEAD_DIM = 64
HALF = HEAD_DIM // 2
DIL_GROUPS = ((128, 1), (512, 4), (2048, 16))
N_DIL = len(DIL_GROUPS)
DIL_HEADS = 8
SWA_Q_HEADS = 8
SWA_KV_HEADS = 2
BLOCK = 128
ROPE_THETA = 10000.0
X_HEADS = 4
X_HEAD_DIM = D_MODEL // X_HEADS
D_FF = 4 * D_MODEL
EPS = 1e-6
N_BRANCH = 2

DIL_WIDTH = N_DIL * DIL_HEADS * HEAD_DIM
DIL_OUT = DIL_HEADS * HEAD_DIM
SWA_Q_WIDTH = SWA_Q_HEADS * HEAD_DIM
SWA_KV_WIDTH = SWA_KV_HEADS * HEAD_DIM
GATE_WIDTH = N_BRANCH * D_MODEL

LANES = 128
HG = 4
HGW = HG * HEAD_DIM
GROUP_W = 2 * HGW
QKV_W = 3 * GROUP_W

COL_GATE = 0
COL_D1 = GATE_WIDTH
COL_SWA = COL_D1 + QKV_W
NAT_W = COL_SWA + QKV_W
PROJ_W = NAT_W + (N_DIL - 1) * QKV_W

TM = 512
TP = 1024
SUB = 512
QT = 2048
NEG = -1e30
VMEM_LIMIT = 56 * 1024 * 1024


def _const_spec(shape):
    nd = len(shape)
    return pl.BlockSpec(shape, lambda *_: (0,) * nd, pipeline_mode=pl.Buffered(1))


def _params(n_axes):
    return pltpu.CompilerParams(dimension_semantics=("parallel",) * n_axes,
                                vmem_limit_bytes=VMEM_LIMIT)


def _rms(x, g):
    ms = jnp.mean(x * x, axis=-1, keepdims=True)
    return x * lax.rsqrt(ms + EPS) * g


def _rope(r, cos, sin):
    a = r[:, :LANES]
    b = r[:, LANES:]
    return jnp.concatenate([a * cos - b * sin, b * cos + a * sin], axis=1)


def _proj_kernel(x_ref, g_ref, w_ref, bg_ref, pos_ref, inv_ref, nat_ref, *rest):
    res_refs = rest[:N_DIL - 1]
    hs_ref, hp_ref, cos_ref, sin_ref = rest[N_DIL - 1:]
    h = _rms(x_ref[...], g_ref[...])
    hn = h.astype(jnp.bfloat16)
    ang = pos_ref[...] * inv_ref[...]
    lane_grp = lax.broadcasted_iota(jnp.int32, (1, LANES), 1) // HALF
    for trig, t_ref in ((jnp.cos, cos_ref), (jnp.sin, sin_ref)):
        t = trig(ang)
        rolled = [t] + [pltpu.roll(t, HALF * k, 1) for k in range(1, HG)]
        for q in range(HG):
            rep = rolled[(0 - q) % HG]
            for k in range(1, HG):
                rep = jnp.where(lane_grp == k, rolled[(k - q) % HG], rep)
            t_ref[pl.ds(q, TM // HG, stride=HG), :] = rep
    cos = cos_ref[...]
    sin = sin_ref[...]

    for j in range(D_MODEL // LANES):
        hs_ref[j] = h[:, j * LANES:(j + 1) * LANES]
    for gi in range(N_DIL - 1):
        d = DIL_GROUPS[gi + 1][1]
        tmd = TM // d
        for r in range(d):
            for j in range(D_MODEL // LANES):
                hp_ref[gi, r * tmd:(r + 1) * tmd, j * LANES:(j + 1) * LANES] = (
                    hs_ref[j, pl.ds(r, tmd, stride=d), :].astype(jnp.bfloat16))

    for c in range(0, NAT_W, HGW):
        r = jnp.dot(hn, w_ref[:, c:c + HGW], preferred_element_type=jnp.float32)
        if c < COL_D1:
            z = r + bg_ref[:, c:c + HGW]
            out = 0.5 * jnp.tanh(0.5 * z) + 0.5
        elif (c - COL_D1) % QKV_W < 2 * GROUP_W:
            out = _rope(r, cos, sin)
        else:
            out = r
        nat_ref[:, c:c + HGW] = out.astype(jnp.bfloat16)

    for gi, o_ref in enumerate(res_refs):
        d = DIL_GROUPS[gi + 1][1]
        tmd = TM // d
        cosp = jnp.concatenate([cos_ref[pl.ds(r, tmd, stride=d), :] for r in range(d)], axis=0)
        sinp = jnp.concatenate([sin_ref[pl.ds(r, tmd, stride=d), :] for r in range(d)], axis=0)
        hp = hp_ref[gi]
        col0 = NAT_W + gi * QKV_W
        for c in range(0, QKV_W, HGW):
            rr = jnp.dot(hp, w_ref[:, col0 + c:col0 + c + HGW],
                         preferred_element_type=jnp.float32)
            out = (_rope(rr, cosp, sinp) if c < 2 * GROUP_W else rr).astype(jnp.bfloat16)
            for r in range(d):
                o_ref[r, :, c:c + HGW] = out[r * tmd:(r + 1) * tmd]


def _proj(x, g, w, bg, pos, inv4):
    B, S, _ = x.shape
    tok = lambda w_: pl.BlockSpec((None, TM, w_), lambda b, i: (b, i, 0))
    out_specs = [tok(NAT_W)]
    out_shape = [jax.ShapeDtypeStruct((B, S, NAT_W), jnp.bfloat16)]
    for _, d in DIL_GROUPS[1:]:
        out_specs.append(pl.BlockSpec((None, d, TM // d, QKV_W), lambda b, i: (b, 0, i, 0)))
        out_shape.append(jax.ShapeDtypeStruct((B, d, S // d, QKV_W), jnp.bfloat16))
    return pl.pallas_call(
        _proj_kernel,
        name="proj",
        grid=(B, S // TM),
        in_specs=[tok(D_MODEL), _const_spec((1, D_MODEL)), _const_spec((D_MODEL, PROJ_W)),
                  _const_spec((1, GATE_WIDTH)),
                  pl.BlockSpec((None, TM // HG, LANES), lambda b, i: (b, i, 0)),
                  _const_spec((1, LANES))],
        out_specs=out_specs,
        out_shape=out_shape,
        scratch_shapes=[pltpu.VMEM((D_MODEL // LANES, TM, LANES), jnp.float32),
                        pltpu.VMEM((N_DIL - 1, TM, D_MODEL), jnp.bfloat16),
                        pltpu.VMEM((TM, LANES), jnp.float32),
                        pltpu.VMEM((TM, LANES), jnp.float32)],
        compiler_params=_params(2),
    )(x, g, w, bg, pos, inv4)


def _attn_kernel(sink_ref, q_ref, kc_ref, kp_ref, vc_ref, vp_ref, o_ref, *rest,
                 has_sink, write_lse):
    lse_ref = rest[0] if write_lse else None
    first = pl.program_id(1) == 0
    nseq, qt, _ = q_ref.shape

    row = lax.broadcasted_iota(jnp.int32, (BLOCK, 2 * BLOCK), 0)
    col = lax.broadcasted_iota(jnp.int32, (BLOCK, 2 * BLOCK), 1)
    band = (col >= row) & (col <= row + BLOCK)
    bias = jnp.where(band, 0.0, NEG).astype(jnp.float32)
    bias_first = jnp.where(band & ((col >= BLOCK) | jnp.logical_not(first)), 0.0, NEG
                           ).astype(jnp.float32)

    lane = lax.broadcasted_iota(jnp.int32, (1, HGW), 1)
    qmask = [jnp.where((lane % LANES) // HALF == j, 1.0, 0.0).astype(jnp.bfloat16)
             for j in range(HG)]
    vhead = lane // HEAD_DIM
    lse_lane = lax.broadcasted_iota(jnp.int32, (BLOCK, LANES), 1)

    for n in range(nseq):
        for qb in range(qt // BLOCK):
            rows = slice(qb * BLOCK, (qb + 1) * BLOCK)
            lse_tile = jnp.zeros((BLOCK, LANES), jnp.float32)
            for g in range(2):
                cs = slice(g * HGW, (g + 1) * HGW)
                if qb == 0:
                    kcat = jnp.concatenate([kp_ref[n, :, cs], kc_ref[n, rows, cs]], axis=0)
                    vcat = jnp.concatenate([vp_ref[n, :, cs], vc_ref[n, rows, cs]], axis=0)
                else:
                    kcat = kc_ref[n, (qb - 1) * BLOCK:(qb + 1) * BLOCK, cs]
                    vcat = vc_ref[n, (qb - 1) * BLOCK:(qb + 1) * BLOCK, cs]
                bias_q = bias_first if qb == 0 else bias
                q4 = q_ref[n, rows, cs]
                qs = jnp.concatenate([q4 * qmask[j] for j in range(HG)], axis=0)
                s_all = lax.dot_general(qs, kcat, (((1,), (1,)), ((), ())),
                                        preferred_element_type=jnp.float32)
                ps = []
                for j in range(HG):
                    s = s_all[j * BLOCK:(j + 1) * BLOCK] + bias_q
                    m = jnp.max(s, axis=-1, keepdims=True)
                    if has_sink:
                        sk = sink_ref[g * HG + j]
                        m = jnp.maximum(m, sk)
                    p = jnp.exp(s - m)
                    den = jnp.sum(p, axis=-1, keepdims=True)
                    if has_sink:
                        den = den + jnp.exp(sk - m)
                    ps.append((p * (1.0 / den)).astype(jnp.bfloat16))
                    if write_lse:
                        lse = m + jnp.log(den)
                        lse_tile = jnp.where(lse_lane == g * HG + j, lse, lse_tile)
                o_s = jnp.dot(jnp.concatenate(ps, axis=0), vcat,
                              preferred_element_type=jnp.float32)
                o4 = o_s[:BLOCK]
                for j in range(1, HG):
                    o4 = jnp.where(vhead == j, o_s[j * BLOCK:(j + 1) * BLOCK], o4)
                o_ref[n, rows, cs] = o4.astype(jnp.bfloat16)
            if write_lse:
                lse_ref[n, rows, :] = lse_tile


def _attn(qkv, cb0, sink, *, has_sink, write_lse):
    N, L, _ = qkv.shape
    qt = min(QT, L)
    ns = QT // qt
    r = qt // BLOCK
    cur = lambda cb: pl.BlockSpec((ns, qt, GROUP_W), lambda n, i: (n, i, cb))
    prev = lambda cb: pl.BlockSpec((ns, BLOCK, GROUP_W),
                                   lambda n, i: (n, jnp.maximum(i * r - 1, 0), cb))
    out_specs = [pl.BlockSpec((ns, qt, GROUP_W), lambda n, i: (n, i, 0))]
    out_shape = [jax.ShapeDtypeStruct((N, L, GROUP_W), jnp.bfloat16)]
    if write_lse:
        out_specs.append(pl.BlockSpec((ns, qt, LANES), lambda n, i: (n, i, 0)))
        out_shape.append(jax.ShapeDtypeStruct((N, L, LANES), jnp.float32))
    res = pl.pallas_call(
        functools.partial(_attn_kernel, has_sink=has_sink, write_lse=write_lse),
        name="attn_sink" if has_sink else "attn",
        grid=(N // ns, L // qt),
        in_specs=[pl.BlockSpec(memory_space=pltpu.SMEM),
                  cur(cb0), cur(cb0 + 1), prev(cb0 + 1), cur(cb0 + 2), prev(cb0 + 2)],
        out_specs=out_specs,
        out_shape=out_shape,
        compiler_params=_params(2),
    )(sink, qkv, qkv, qkv, qkv, qkv)
    return res if write_lse else res[0]


def _expand_matrix():
    e = np.zeros((2 * LANES, N_DIL * DIL_OUT), np.float32)
    for g in range(N_DIL):
        for hh in range(DIL_HEADS):
            c0 = g * DIL_OUT + hh * HEAD_DIM
            e[g * DIL_HEADS + hh, c0:c0 + HEAD_DIM] = 1.0
            e[LANES + g * DIL_HEADS + hh, c0:c0 + HEAD_DIM] = 1.0
    return jnp.asarray(e, jnp.bfloat16)


def _merge_kernel(x_ref, o0_ref, o1_ref, o2_ref, l0_ref, l1_ref, l2_ref, ob_ref, gt_ref,
                  e_ref, wa_ref, wb_ref, wo_ref, out_ref, onat_ref, lnat_ref):
    res = ((DIL_GROUPS[1][1], o1_ref, l1_ref), (DIL_GROUPS[2][1], o2_ref, l2_ref))
    nslab = DIL_OUT // LANES
    for gi, (d, o_ref, l_ref) in enumerate(res):
        for r in range(d):
            lnat_ref[gi, pl.ds(r, TP // d, stride=d), :] = l_ref[r]
            of = o_ref[r].astype(jnp.float32)
            for s in range(nslab):
                onat_ref[gi * nslab + s, pl.ds(r, TP // d, stride=d), :] = (
                    of[:, s * LANES:(s + 1) * LANES])
    lane = lax.broadcasted_iota(jnp.int32, (1, LANES), 1)
    for r0 in range(0, TP, SUB):
        rows = slice(r0, r0 + SUB)
        ls = [l0_ref[rows, :], lnat_ref[0, rows, :], lnat_ref[1, rows, :]]
        mx = jnp.maximum(jnp.maximum(ls[0], ls[1]), ls[2])
        es = [jnp.exp(l - mx) for l in ls]
        inv = 1.0 / (es[0] + es[1] + es[2])
        a = jnp.zeros((SUB, LANES), jnp.float32)
        for gi, e in enumerate(es):
            ag = jnp.where(lane < DIL_HEADS, e * inv, 0.0)
            a = a + (pltpu.roll(ag, gi * DIL_HEADS, 1) if gi else ag)
        a_hi = a.astype(jnp.bfloat16)
        a_lo = (a - a_hi.astype(jnp.float32)).astype(jnp.bfloat16)
        aa = jnp.concatenate([a_hi, a_lo], axis=1)
        oa = jnp.dot(aa, e_ref[:, :DIL_OUT], preferred_element_type=jnp.float32
                     ) * o0_ref[rows, :].astype(jnp.float32)
        for gi in range(len(res)):
            og = jnp.concatenate([onat_ref[gi * nslab + s, rows, :] for s in range(nslab)],
                                 axis=1)
            wide = jnp.dot(aa, e_ref[:, (gi + 1) * DIL_OUT:(gi + 2) * DIL_OUT],
                           preferred_element_type=jnp.float32)
            oa = oa + wide * og
        ya = jnp.dot(oa.astype(jnp.bfloat16), wa_ref[...], preferred_element_type=jnp.float32)
        yb = jnp.dot(ob_ref[rows, :], wb_ref[...], preferred_element_type=jnp.float32)
        merged = (gt_ref[rows, :D_MODEL].astype(jnp.float32) * ya
                  + gt_ref[rows, D_MODEL:].astype(jnp.float32) * yb)
        out_ref[rows, :] = x_ref[rows, :] + jnp.dot(merged.astype(jnp.bfloat16), wo_ref[...],
                                                    preferred_element_type=jnp.float32)


def _merge(x, nat, o_dil, lse_dil, o_swa, wa, wb, wo):
    B, S, _ = x.shape
    tok = lambda w, cb=0: pl.BlockSpec((None, TP, w), lambda b, i: (b, i, cb))
    res = lambda d, w: pl.BlockSpec((None, d, TP // d, w), lambda b, i: (b, 0, i, 0))
    d1, d2 = DIL_GROUPS[1][1], DIL_GROUPS[2][1]
    return pl.pallas_call(
        _merge_kernel,
        name="merge",
        grid=(B, S // TP),
        in_specs=[tok(D_MODEL), tok(GROUP_W), res(d1, GROUP_W), res(d2, GROUP_W),
                  tok(LANES), res(d1, LANES), res(d2, LANES),
                  tok(GROUP_W), tok(GATE_WIDTH, COL_GATE // GATE_WIDTH),
                  _const_spec((2 * LANES, N_DIL * DIL_OUT)),
                  _const_spec((DIL_OUT, D_MODEL)), _const_spec((SWA_Q_WIDTH, D_MODEL)),
                  _const_spec((D_MODEL, D_MODEL))],
        out_specs=tok(D_MODEL),
        out_shape=jax.ShapeDtypeStruct((B, S, D_MODEL), jnp.float32),
        scratch_shapes=[pltpu.VMEM(((N_DIL - 1) * DIL_OUT // LANES, TP, LANES), jnp.float32),
                        pltpu.VMEM((N_DIL - 1, TP, LANES), jnp.float32)],
        compiler_params=_params(2),
    )(x, o_dil[0], o_dil[1].reshape(B, d1, S // d1, GROUP_W),
      o_dil[2].reshape(B, d2, S // d2, GROUP_W),
      lse_dil[0], lse_dil[1].reshape(B, d1, S // d1, LANES),
      lse_dil[2].reshape(B, d2, S // d2, LANES),
      o_swa, nat, _expand_matrix(), wa, wb, wo)


def _memkv_kernel(mem_ref, g_ref, w_ref, kv_ref):
    mn = _rms(mem_ref[...], g_ref[...]).astype(jnp.bfloat16)
    kv_ref[...] = jnp.dot(mn, w_ref[...], preferred_element_type=jnp.float32
                          ).astype(jnp.bfloat16)


def _memkv(mem, g, w):
    B, M, _ = mem.shape
    return pl.pallas_call(
        _memkv_kernel,
        name="memkv",
        grid=(B,),
        in_specs=[pl.BlockSpec((None, M, D_MODEL), lambda b: (b, 0, 0)),
                  _const_spec((1, D_MODEL)), _const_spec((D_MODEL, 2 * D_MODEL))],
        out_specs=pl.BlockSpec((None, M, 2 * D_MODEL), lambda b: (b, 0, 0)),
        out_shape=jax.ShapeDtypeStruct((B, M, 2 * D_MODEL), jnp.bfloat16),
        compiler_params=_params(1),
    )(mem, g, w)


def _cross_kernel(x_ref, g_ref, wq_ref, kv_ref, wo_ref, out_ref):
    for r0 in range(0, TP, SUB):
        rows = slice(r0, r0 + SUB)
        x = x_ref[rows, :]
        hc = _rms(x, g_ref[...]).astype(jnp.bfloat16)
        q = jnp.dot(hc, wq_ref[...], preferred_element_type=jnp.float32).astype(jnp.bfloat16)
        os = []
        for hh in range(X_HEADS):
            cs = slice(hh * X_HEAD_DIM, (hh + 1) * X_HEAD_DIM)
            s = lax.dot_general(q[:, cs], kv_ref[:, cs], (((1,), (1,)), ((), ())),
                                preferred_element_type=jnp.float32)
            m = jnp.max(s, axis=-1, keepdims=True)
            p = jnp.exp(s - m)
            p = p * (1.0 / jnp.sum(p, axis=-1, keepdims=True))
            vs = slice(D_MODEL + hh * X_HEAD_DIM, D_MODEL + (hh + 1) * X_HEAD_DIM)
            os.append(jnp.dot(p.astype(jnp.bfloat16), kv_ref[:, vs],
                              preferred_element_type=jnp.float32).astype(jnp.bfloat16))
        o = jnp.concatenate(os, axis=1)
        out_ref[rows, :] = x + jnp.dot(o, wo_ref[...], preferred_element_type=jnp.float32)


def _cross(x, g, wq, kv, wo):
    B, S, _ = x.shape
    M = kv.shape[1]
    tok = pl.BlockSpec((None, TP, D_MODEL), lambda b, i: (b, i, 0))
    return pl.pallas_call(
        _cross_kernel,
        name="cross",
        grid=(B, S // TP),
        in_specs=[tok, _const_spec((1, D_MODEL)), _const_spec((D_MODEL, D_MODEL)),
                  pl.BlockSpec((None, M, 2 * D_MODEL), lambda b, i: (b, 0, 0)),
                  _const_spec((D_MODEL, D_MODEL))],
        out_specs=tok,
        out_shape=jax.ShapeDtypeStruct((B, S, D_MODEL), jnp.float32),
        compiler_params=_params(2),
    )(x, g, wq, kv, wo)


FF_CHUNK = 1024


def _mlp_kernel(x_ref, g_ref, w1_ref, w2_ref, gf_ref, out_ref, *, final_norm):
    x = x_ref[...]
    h = _rms(x, g_ref[...]).astype(jnp.bfloat16)
    y = x
    for c in range(0, D_FF, FF_CHUNK):
        a = jnp.maximum(jnp.dot(h, w1_ref[:, c:c + FF_CHUNK],
                                preferred_element_type=jnp.float32), 0.0)
        y = y + jnp.dot((a * a).astype(jnp.bfloat16), w2_ref[c:c + FF_CHUNK, :],
                        preferred_element_type=jnp.float32)
    out_ref[...] = _rms(y, gf_ref[...]) if final_norm else y


def _mlp(x, g, w1, w2, gf, final_norm):
    B, S, _ = x.shape
    tok = pl.BlockSpec((None, TP, D_MODEL), lambda b, i: (b, i, 0))
    return pl.pallas_call(
        functools.partial(_mlp_kernel, final_norm=final_norm),
        name="mlp",
        grid=(B, S // TP),
        in_specs=[tok, _const_spec((1, D_MODEL)), _const_spec((D_MODEL, D_FF)),
                  _const_spec((D_FF, D_MODEL)), _const_spec((1, D_MODEL))],
        out_specs=tok,
        out_shape=jax.ShapeDtypeStruct((B, S, D_MODEL), jnp.float32),
        compiler_params=_params(2),
    )(x, g, w1, w2, gf)


def _split_half(w, n_heads):
    k = w.shape[0]
    w = w.reshape(k, n_heads // HG, HG, 2, HALF)
    return w.transpose(0, 1, 3, 2, 4).reshape(k, n_heads * HEAD_DIM)


def _permute_w_in(w_in):
    scale = HEAD_DIM ** -0.5
    c = 0
    qa = w_in[:, c:c + DIL_WIDTH]; c += DIL_WIDTH
    ka = w_in[:, c:c + DIL_WIDTH]; c += DIL_WIDTH
    va = w_in[:, c:c + DIL_WIDTH]; c += DIL_WIDTH
    qb = w_in[:, c:c + SWA_Q_WIDTH]; c += SWA_Q_WIDTH
    kb = w_in[:, c:c + SWA_KV_WIDTH]; c += SWA_KV_WIDTH
    vb = w_in[:, c:c + SWA_KV_WIDTH]; c += SWA_KV_WIDTH
    gt = w_in[:, c:]
    kd = w_in.shape[0]

    def dil(g):
        sl = slice(g * GROUP_W, (g + 1) * GROUP_W)
        return [_split_half(qa[:, sl] * scale, DIL_HEADS), _split_half(ka[:, sl], DIL_HEADS),
                va[:, sl]]

    kb = kb.reshape(kd, SWA_KV_HEADS, 2, 1, HALF)
    kb = jnp.broadcast_to(kb, (kd, SWA_KV_HEADS, 2, HG, HALF)).reshape(kd, GROUP_W)
    vb = vb.reshape(kd, SWA_KV_HEADS, 1, HEAD_DIM)
    vb = jnp.broadcast_to(vb, (kd, SWA_KV_HEADS, HG, HEAD_DIM)).reshape(kd, GROUP_W)
    cols = [gt] + dil(0) + [_split_half(qb * scale, SWA_Q_HEADS), kb, vb]
    for g in range(1, N_DIL):
        cols += dil(g)
    return jnp.concatenate(cols, axis=1).astype(jnp.bfloat16)


def kernel(x, mem, positions, g_mix, w_in, b_gate, sink, w_branch_a, w_branch_b, w_out,
           g_cross, g_mem, w_cq, w_ckv, w_co, g_mlp, w_1, w_2, g_final):
    bf = jnp.bfloat16
    B, S, _ = x.shape
    depth = g_mix.shape[0]
    inv = ROPE_THETA ** (-jnp.arange(HALF, dtype=jnp.float32) / HALF)
    inv4 = jnp.tile(inv, HG)[None]
    pos = jnp.repeat(positions.astype(jnp.float32).reshape(B, S // HG, HG), HALF, axis=-1)
    for l in range(depth):
        nat, *res = _proj(x, g_mix[l][None], _permute_w_in(w_in[l]),
                          b_gate[l].reshape(1, GATE_WIDTH), pos, inv4)
        o_dil, lse_dil = [], []
        for gi, (_, d) in enumerate(DIL_GROUPS):
            if gi == 0:
                o, lse = _attn(nat, COL_D1 // GROUP_W, sink[l], has_sink=False, write_lse=True)
            else:
                o, lse = _attn(res[gi - 1].reshape(B * d, S // d, QKV_W), 0, sink[l],
                               has_sink=False, write_lse=True)
            o_dil.append(o)
            lse_dil.append(lse)
        o_swa = _attn(nat, COL_SWA // GROUP_W, sink[l], has_sink=True, write_lse=False)
        x = _merge(x, nat, o_dil, lse_dil, o_swa, w_branch_a[l].astype(bf),
                   w_branch_b[l].astype(bf), w_out[l].astype(bf))
        kv = _memkv(mem, g_mem[l][None], w_ckv[l].astype(bf))
        x = _cross(x, g_cross[l][None], (w_cq[l] * (X_HEAD_DIM ** -0.5)).astype(bf), kv,
                   w_co[l].astype(bf))
        x = _mlp(x, g_mlp[l][None], w_1[l].astype(bf), w_2[l].astype(bf), g_final[None],
                 final_norm=(l == depth - 1))
    return x
```

```python
import functools

import numpy as np
import jax
import jax.numpy as jnp
from jax import lax
from jax.experimental import pallas as pl
from jax.experimental.pallas import tpu as pltpu

D_MODEL = 1024
HEAD_DIM = 64
HALF = HEAD_DIM // 2
DIL_GROUPS = ((128, 1), (512, 4), (2048, 16))
N_DIL = len(DIL_GROUPS)
DIL_HEADS = 8
SWA_Q_HEADS = 8
SWA_KV_HEADS = 2
BLOCK = 128
ROPE_THETA = 10000.0
X_HEADS = 4
X_HEAD_DIM = D_MODEL // X_HEADS
D_FF = 4 * D_MODEL
EPS = 1e-6
N_BRANCH = 2

DIL_WIDTH = N_DIL * DIL_HEADS * HEAD_DIM
DIL_OUT = DIL_HEADS * HEAD_DIM
SWA_Q_WIDTH = SWA_Q_HEADS * HEAD_DIM
SWA_KV_WIDTH = SWA_KV_HEADS * HEAD_DIM
GATE_WIDTH = N_BRANCH * D_MODEL

LANES = 128
HG = 4
HGW = HG * HEAD_DIM
GROUP_W = 2 * HGW
QKV_W = 3 * GROUP_W

COL_GATE = 0
COL_D1 = GATE_WIDTH
COL_SWA = COL_D1 + QKV_W
NAT_W = COL_SWA + QKV_W
PROJ_W = NAT_W + (N_DIL - 1) * QKV_W

TM = 512
TP = 1024
SUB = 512
QT = 4096
NEG = -1e30
VMEM_LIMIT = 56 * 1024 * 1024


def _const_spec(shape):
    nd = len(shape)
    return pl.BlockSpec(shape, lambda *_: (0,) * nd, pipeline_mode=pl.Buffered(1))


def _params(n_axes):
    return pltpu.CompilerParams(dimension_semantics=("parallel",) * n_axes,
                                vmem_limit_bytes=VMEM_LIMIT)


def _rms(x, g):
    ms = jnp.mean(x * x, axis=-1, keepdims=True)
    return x * lax.rsqrt(ms + EPS) * g


def _rope(r, cos, sin):
    a = r[:, :LANES]
    b = r[:, LANES:]
    return jnp.concatenate([a * cos - b * sin, b * cos + a * sin], axis=1)


def _proj_kernel(x_ref, g_ref, w_ref, bg_ref, pos_ref, inv_ref, nat_ref, *rest):
    res_refs = rest[:N_DIL - 1]
    hs_ref, hp_ref, cos_ref, sin_ref = rest[N_DIL - 1:]
    h = _rms(x_ref[...], g_ref[...])
    hn = h.astype(jnp.bfloat16)
    ang = pos_ref[...] * inv_ref[...]
    lane_grp = lax.broadcasted_iota(jnp.int32, (1, LANES), 1) // HALF
    for trig, t_ref in ((jnp.cos, cos_ref), (jnp.sin, sin_ref)):
        t = trig(ang)
        rolled = [t] + [pltpu.roll(t, HALF * k, 1) for k in range(1, HG)]
        for q in range(HG):
            rep = rolled[(0 - q) % HG]
            for k in range(1, HG):
                rep = jnp.where(lane_grp == k, rolled[(k - q) % HG], rep)
            t_ref[pl.ds(q, TM // HG, stride=HG), :] = rep
    cos = cos_ref[...]
    sin = sin_ref[...]

    for j in range(D_MODEL // LANES):
        hs_ref[j] = h[:, j * LANES:(j + 1) * LANES]
    for gi in range(N_DIL - 1):
        d = DIL_GROUPS[gi + 1][1]
        tmd = TM // d
        for r in range(d):
            for j in range(D_MODEL // LANES):
                hp_ref[gi, r * tmd:(r + 1) * tmd, j * LANES:(j + 1) * LANES] = (
                    hs_ref[j, pl.ds(r, tmd, stride=d), :].astype(jnp.bfloat16))

    for c in range(0, NAT_W, HGW):
        r = jnp.dot(hn, w_ref[:, c:c + HGW], preferred_element_type=jnp.float32)
        if c < COL_D1:
            z = r + bg_ref[:, c:c + HGW]
            out = 0.5 * jnp.tanh(0.5 * z) + 0.5
        elif (c - COL_D1) % QKV_W < 2 * GROUP_W:
            out = _rope(r, cos, sin)
        else:
            out = r
        nat_ref[:, c:c + HGW] = out.astype(jnp.bfloat16)

    for gi, o_ref in enumerate(res_refs):
        d = DIL_GROUPS[gi + 1][1]
        tmd = TM // d
        cosp = jnp.concatenate([cos_ref[pl.ds(r, tmd, stride=d), :] for r in range(d)], axis=0)
        sinp = jnp.concatenate([sin_ref[pl.ds(r, tmd, stride=d), :] for r in range(d)], axis=0)
        hp = hp_ref[gi]
        col0 = NAT_W + gi * QKV_W
        for c in range(0, QKV_W, HGW):
            rr = jnp.dot(hp, w_ref[:, col0 + c:col0 + c + HGW],
                         preferred_element_type=jnp.float32)
            out = (_rope(rr, cosp, sinp) if c < 2 * GROUP_W else rr).astype(jnp.bfloat16)
            for r in range(d):
                o_ref[r, :, c:c + HGW] = out[r * tmd:(r + 1) * tmd]


def _proj(x, g, w, bg, pos, inv4):
    B, S, _ = x.shape
    tok = lambda w_: pl.BlockSpec((None, TM, w_), lambda b, i: (b, i, 0))
    out_specs = [tok(NAT_W)]
    out_shape = [jax.ShapeDtypeStruct((B, S, NAT_W), jnp.bfloat16)]
    for _, d in DIL_GROUPS[1:]:
        out_specs.append(pl.BlockSpec((None, d, TM // d, QKV_W), lambda b, i: (b, 0, i, 0)))
        out_shape.append(jax.ShapeDtypeStruct((B, d, S // d, QKV_W), jnp.bfloat16))
    return pl.pallas_call(
        _proj_kernel,
        name="proj",
        grid=(B, S // TM),
        in_specs=[tok(D_MODEL), _const_spec((1, D_MODEL)), _const_spec((D_MODEL, PROJ_W)),
                  _const_spec((1, GATE_WIDTH)),
                  pl.BlockSpec((None, TM // HG, LANES), lambda b, i: (b, i, 0)),
                  _const_spec((1, LANES))],
        out_specs=out_specs,
        out_shape=out_shape,
        scratch_shapes=[pltpu.VMEM((D_MODEL // LANES, TM, LANES), jnp.float32),
                        pltpu.VMEM((N_DIL - 1, TM, D_MODEL), jnp.bfloat16),
                        pltpu.VMEM((TM, LANES), jnp.float32),
                        pltpu.VMEM((TM, LANES), jnp.float32)],
        compiler_params=_params(2),
    )(x, g, w, bg, pos, inv4)


def _attn_kernel(sink_ref, q_ref, kc_ref, kp_ref, vc_ref, vp_ref, o_ref, *rest,
                 has_sink, write_lse):
    lse_ref = rest[0] if write_lse else None
    first = pl.program_id(1) == 0
    nseq, qt, _ = q_ref.shape

    row = lax.broadcasted_iota(jnp.int32, (BLOCK, 2 * BLOCK), 0)
    col = lax.broadcasted_iota(jnp.int32, (BLOCK, 2 * BLOCK), 1)
    band = (col >= row) & (col <= row + BLOCK)
    bias = jnp.where(band, 0.0, NEG).astype(jnp.float32)
    bias_first = jnp.where(band & ((col >= BLOCK) | jnp.logical_not(first)), 0.0, NEG
                           ).astype(jnp.float32)

    lane = lax.broadcasted_iota(jnp.int32, (1, HGW), 1)
    qmask = [jnp.where((lane % LANES) // HALF == j, 1.0, 0.0).astype(jnp.bfloat16)
             for j in range(HG)]
    vhead = lane // HEAD_DIM
    lse_lane = lax.broadcasted_iota(jnp.int32, (BLOCK, LANES), 1)

    for n in range(nseq):
        for qb in range(qt // BLOCK):
            rows = slice(qb * BLOCK, (qb + 1) * BLOCK)
            lse_tile = jnp.zeros((BLOCK, LANES), jnp.float32)
            for g in range(2):
                cs = slice(g * HGW, (g + 1) * HGW)
                if qb == 0:
                    kcat = jnp.concatenate([kp_ref[n, :, cs], kc_ref[n, rows, cs]], axis=0)
                    vcat = jnp.concatenate([vp_ref[n, :, cs], vc_ref[n, rows, cs]], axis=0)
                else:
                    kcat = kc_ref[n, (qb - 1) * BLOCK:(qb + 1) * BLOCK, cs]
                    vcat = vc_ref[n, (qb - 1) * BLOCK:(qb + 1) * BLOCK, cs]
                bias_q = bias_first if qb == 0 else bias
                q4 = q_ref[n, rows, cs]
                qs = jnp.concatenate([q4 * qmask[j] for j in range(HG)], axis=0)
                s_all = lax.dot_general(qs, kcat, (((1,), (1,)), ((), ())),
                                        preferred_element_type=jnp.float32)
                ps = []
                for j in range(HG):
                    s = s_all[j * BLOCK:(j + 1) * BLOCK] + bias_q
                    m = jnp.max(s, axis=-1, keepdims=True)
                    if has_sink:
                        sk = sink_ref[g * HG + j]
                        m = jnp.maximum(m, sk)
                    p = jnp.exp(s - m)
                    den = jnp.sum(p, axis=-1, keepdims=True)
                    if has_sink:
                        den = den + jnp.exp(sk - m)
                    ps.append((p * (1.0 / den)).astype(jnp.bfloat16))
                    if write_lse:
                        lse = m + jnp.log(den)
                        lse_tile = jnp.where(lse_lane == g * HG + j, lse, lse_tile)
                o_s = jnp.dot(jnp.concatenate(ps, axis=0), vcat,
                              preferred_element_type=jnp.float32)
                o4 = o_s[:BLOCK]
                for j in range(1, HG):
                    o4 = jnp.where(vhead == j, o_s[j * BLOCK:(j + 1) * BLOCK], o4)
                o_ref[n, rows, cs] = o4.astype(jnp.bfloat16)
            if write_lse:
                lse_ref[n, rows, :] = lse_tile


def _attn(qkv, cb0, sink, *, has_sink, write_lse):
    N, L, _ = qkv.shape
    qt = min(QT, L)
    ns = QT // qt
    r = qt // BLOCK
    cur = lambda cb: pl.BlockSpec((ns, qt, GROUP_W), lambda n, i: (n, i, cb))
    prev = lambda cb: pl.BlockSpec((ns, BLOCK, GROUP_W),
                                   lambda n, i: (n, jnp.maximum(i * r - 1, 0), cb))
    out_specs = [pl.BlockSpec((ns, qt, GROUP_W), lambda n, i: (n, i, 0))]
    out_shape = [jax.ShapeDtypeStruct((N, L, GROUP_W), jnp.bfloat16)]
    if write_lse:
        out_specs.append(pl.BlockSpec((ns, qt, LANES), lambda n, i: (n, i, 0)))
        out_shape.append(jax.ShapeDtypeStruct((N, L, LANES), jnp.float32))
    res = pl.pallas_call(
        functools.partial(_attn_kernel, has_sink=has_sink, write_lse=write_lse),
        name="attn_sink" if has_sink else "attn",
        grid=(N // ns, L // qt),
        in_specs=[pl.BlockSpec(memory_space=pltpu.SMEM),
                  cur(cb0), cur(cb0 + 1), prev(cb0 + 1), cur(cb0 + 2), prev(cb0 + 2)],
        out_specs=out_specs,
        out_shape=out_shape,
        compiler_params=_params(2),
    )(sink, qkv, qkv, qkv, qkv, qkv)
    return res if write_lse else res[0]


def _expand_matrix():
    e = np.zeros((2 * LANES, N_DIL * DIL_OUT), np.float32)
    for g in range(N_DIL):
        for hh in range(DIL_HEADS):
            c0 = g * DIL_OUT + hh * HEAD_DIM
            e[g * DIL_HEADS + hh, c0:c0 + HEAD_DIM] = 1.0
            e[LANES + g * DIL_HEADS + hh, c0:c0 + HEAD_DIM] = 1.0
    return jnp.asarray(e, jnp.bfloat16)


def _merge_kernel(x_ref, o0_ref, o1_ref, o2_ref, l0_ref, l1_ref, l2_ref, ob_ref, gt_ref,
                  e_ref, wa_ref, wb_ref, wo_ref, out_ref, onat_ref, lnat_ref):
    res = ((DIL_GROUPS[1][1], o1_ref, l1_ref), (DIL_GROUPS[2][1], o2_ref, l2_ref))
    nslab = DIL_OUT // LANES
    for gi, (d, o_ref, l_ref) in enumerate(res):
        for r in range(d):
            lnat_ref[gi, pl.ds(r, TP // d, stride=d), :] = l_ref[r]
            of = o_ref[r].astype(jnp.float32)
            for s in range(nslab):
                onat_ref[gi * nslab + s, pl.ds(r, TP // d, stride=d), :] = (
                    of[:, s * LANES:(s + 1) * LANES])
    lane = lax.broadcasted_iota(jnp.int32, (1, LANES), 1)
    for r0 in range(0, TP, SUB):
        rows = slice(r0, r0 + SUB)
        ls = [l0_ref[rows, :], lnat_ref[0, rows, :], lnat_ref[1, rows, :]]
        mx = jnp.maximum(jnp.maximum(ls[0], ls[1]), ls[2])
        es = [jnp.exp(l - mx) for l in ls]
        inv = 1.0 / (es[0] + es[1] + es[2])
        a = jnp.zeros((SUB, LANES), jnp.float32)
        for gi, e in enumerate(es):
            ag = jnp.where(lane < DIL_HEADS, e * inv, 0.0)
            a = a + (pltpu.roll(ag, gi * DIL_HEADS, 1) if gi else ag)
        a_hi = a.astype(jnp.bfloat16)
        a_lo = (a - a_hi.astype(jnp.float32)).astype(jnp.bfloat16)
        aa = jnp.concatenate([a_hi, a_lo], axis=1)
        oa = jnp.dot(aa, e_ref[:, :DIL_OUT], preferred_element_type=jnp.float32
                     ) * o0_ref[rows, :].astype(jnp.float32)
        for gi in range(len(res)):
            og = jnp.concatenate([onat_ref[gi * nslab + s, rows, :] for s in range(nslab)],
                                 axis=1)
            wide = jnp.dot(aa, e_ref[:, (gi + 1) * DIL_OUT:(gi + 2) * DIL_OUT],
                           preferred_element_type=jnp.float32)
            oa = oa + wide * og
        ya = jnp.dot(oa.astype(jnp.bfloat16), wa_ref[...], preferred_element_type=jnp.float32)
        yb = jnp.dot(ob_ref[rows, :], wb_ref[...], preferred_element_type=jnp.float32)
        merged = (gt_ref[rows, :D_MODEL].astype(jnp.float32) * ya
                  + gt_ref[rows, D_MODEL:].astype(jnp.float32) * yb)
        out_ref[rows, :] = x_ref[rows, :] + jnp.dot(merged.astype(jnp.bfloat16), wo_ref[...],
                                                    preferred_element_type=jnp.float32)


def _merge(x, nat, o_dil, lse_dil, o_swa, wa, wb, wo):
    B, S, _ = x.shape
    tok = lambda w, cb=0: pl.BlockSpec((None, TP, w), lambda b, i: (b, i, cb))
    res = lambda d, w: pl.BlockSpec((None, d, TP // d, w), lambda b, i: (b, 0, i, 0))
    d1, d2 = DIL_GROUPS[1][1], DIL_GROUPS[2][1]
    return pl.pallas_call(
        _merge_kernel,
        name="merge",
        grid=(B, S // TP),
        in_specs=[tok(D_MODEL), tok(GROUP_W), res(d1, GROUP_W), res(d2, GROUP_W),
                  tok(LANES), res(d1, LANES), res(d2, LANES),
                  tok(GROUP_W), tok(GATE_WIDTH, COL_GATE // GATE_WIDTH),
                  _const_spec((2 * LANES, N_DIL * DIL_OUT)),
                  _const_spec((DIL_OUT, D_MODEL)), _const_spec((SWA_Q_WIDTH, D_MODEL)),
                  _const_spec((D_MODEL, D_MODEL))],
        out_specs=tok(D_MODEL),
        out_shape=jax.ShapeDtypeStruct((B, S, D_MODEL), jnp.float32),
        scratch_shapes=[pltpu.VMEM(((N_DIL - 1) * DIL_OUT // LANES, TP, LANES), jnp.float32),
                        pltpu.VMEM((N_DIL - 1, TP, LANES), jnp.float32)],
        compiler_params=_params(2),
    )(x, o_dil[0], o_dil[1].reshape(B, d1, S // d1, GROUP_W),
      o_dil[2].reshape(B, d2, S // d2, GROUP_W),
      lse_dil[0], lse_dil[1].reshape(B, d1, S // d1, LANES),
      lse_dil[2].reshape(B, d2, S // d2, LANES),
      o_swa, nat, _expand_matrix(), wa, wb, wo)


def _memkv_kernel(mem_ref, g_ref, w_ref, kv_ref):
    mn = _rms(mem_ref[...], g_ref[...]).astype(jnp.bfloat16)
    kv_ref[...] = jnp.dot(mn, w_ref[...], preferred_element_type=jnp.float32
                          ).astype(jnp.bfloat16)


def _memkv(mem, g, w):
    B, M, _ = mem.shape
    return pl.pallas_call(
        _memkv_kernel,
        name="memkv",
        grid=(B,),
        in_specs=[pl.BlockSpec((None, M, D_MODEL), lambda b: (b, 0, 0)),
                  _const_spec((1, D_MODEL)), _const_spec((D_MODEL, 2 * D_MODEL))],
        out_specs=pl.BlockSpec((None, M, 2 * D_MODEL), lambda b: (b, 0, 0)),
        out_shape=jax.ShapeDtypeStruct((B, M, 2 * D_MODEL), jnp.bfloat16),
        compiler_params=_params(1),
    )(mem, g, w)


def _cross_kernel(x_ref, g_ref, wq_ref, kv_ref, wo_ref, out_ref):
    for r0 in range(0, TP, SUB):
        rows = slice(r0, r0 + SUB)
        x = x_ref[rows, :]
        hc = _rms(x, g_ref[...]).astype(jnp.bfloat16)
        q = jnp.dot(hc, wq_ref[...], preferred_element_type=jnp.float32).astype(jnp.bfloat16)
        os = []
        for hh in range(X_HEADS):
            cs = slice(hh * X_HEAD_DIM, (hh + 1) * X_HEAD_DIM)
            s = lax.dot_general(q[:, cs], kv_ref[:, cs], (((1,), (1,)), ((), ())),
                                preferred_element_type=jnp.float32)
            m = jnp.max(s, axis=-1, keepdims=True)
            p = jnp.exp(s - m)
            p = p * (1.0 / jnp.sum(p, axis=-1, keepdims=True))
            vs = slice(D_MODEL + hh * X_HEAD_DIM, D_MODEL + (hh + 1) * X_HEAD_DIM)
            os.append(jnp.dot(p.astype(jnp.bfloat16), kv_ref[:, vs],
                              preferred_element_type=jnp.float32).astype(jnp.bfloat16))
        o = jnp.concatenate(os, axis=1)
        out_ref[rows, :] = x + jnp.dot(o, wo_ref[...], preferred_element_type=jnp.float32)


def _cross(x, g, wq, kv, wo):
    B, S, _ = x.shape
    M = kv.shape[1]
    tok = pl.BlockSpec((None, TP, D_MODEL), lambda b, i: (b, i, 0))
    return pl.pallas_call(
        _cross_kernel,
        name="cross",
        grid=(B, S // TP),
        in_specs=[tok, _const_spec((1, D_MODEL)), _const_spec((D_MODEL, D_MODEL)),
                  pl.BlockSpec((None, M, 2 * D_MODEL), lambda b, i: (b, 0, 0)),
                  _const_spec((D_MODEL, D_MODEL))],
        out_specs=tok,
        out_shape=jax.ShapeDtypeStruct((B, S, D_MODEL), jnp.float32),
        compiler_params=_params(2),
    )(x, g, wq, kv, wo)


FF_CHUNK = 1024


def _mlp_kernel(x_ref, g_ref, w1_ref, w2_ref, gf_ref, out_ref, *, final_norm):
    x = x_ref[...]
    h = _rms(x, g_ref[...]).astype(jnp.bfloat16)
    y = x
    for c in range(0, D_FF, FF_CHUNK):
        a = jnp.maximum(jnp.dot(h, w1_ref[:, c:c + FF_CHUNK],
                                preferred_element_type=jnp.float32), 0.0)
        y = y + jnp.dot((a * a).astype(jnp.bfloat16), w2_ref[c:c + FF_CHUNK, :],
                        preferred_element_type=jnp.float32)
    out_ref[...] = _rms(y, gf_ref[...]) if final_norm else y


def _mlp(x, g, w1, w2, gf, final_norm):
    B, S, _ = x.shape
    tok = pl.BlockSpec((None, TP, D_MODEL), lambda b, i: (b, i, 0))
    return pl.pallas_call(
        functools.partial(_mlp_kernel, final_norm=final_norm),
        name="mlp",
        grid=(B, S // TP),
        in_specs=[tok, _const_spec((1, D_MODEL)), _const_spec((D_MODEL, D_FF)),
                  _const_spec((D_FF, D_MODEL)), _const_spec((1, D_MODEL))],
        out_specs=tok,
        out_shape=jax.ShapeDtypeStruct((B, S, D_MODEL), jnp.float32),
        compiler_params=_params(2),
    )(x, g, w1, w2, gf)


def _split_half(w, n_heads):
    k = w.shape[0]
    w = w.reshape(k, n_heads // HG, HG, 2, HALF)
    return w.transpose(0, 1, 3, 2, 4).reshape(k, n_heads * HEAD_DIM)


def _permute_w_in(w_in):
    scale = HEAD_DIM ** -0.5
    c = 0
    qa = w_in[:, c:c + DIL_WIDTH]; c += DIL_WIDTH
    ka = w_in[:, c:c + DIL_WIDTH]; c += DIL_WIDTH
    va = w_in[:, c:c + DIL_WIDTH]; c += DIL_WIDTH
    qb = w_in[:, c:c + SWA_Q_WIDTH]; c += SWA_Q_WIDTH
    kb = w_in[:, c:c + SWA_KV_WIDTH]; c += SWA_KV_WIDTH
    vb = w_in[:, c:c + SWA_KV_WIDTH]; c += SWA_KV_WIDTH
    gt = w_in[:, c:]
    kd = w_in.shape[0]

    def dil(g):
        sl = slice(g * GROUP_W, (g + 1) * GROUP_W)
        return [_split_half(qa[:, sl] * scale, DIL_HEADS), _split_half(ka[:, sl], DIL_HEADS),
                va[:, sl]]

    kb = kb.reshape(kd, SWA_KV_HEADS, 2, 1, HALF)
    kb = jnp.broadcast_to(kb, (kd, SWA_KV_HEADS, 2, HG, HALF)).reshape(kd, GROUP_W)
    vb = vb.reshape(kd, SWA_KV_HEADS, 1, HEAD_DIM)
    vb = jnp.broadcast_to(vb, (kd, SWA_KV_HEADS, HG, HEAD_DIM)).reshape(kd, GROUP_W)
    cols = [gt] + dil(0) + [_split_half(qb * scale, SWA_Q_HEADS), kb, vb]
    for g in range(1, N_DIL):
        cols += dil(g)
    return jnp.concatenate(cols, axis=1).astype(jnp.bfloat16)


def kernel(x, mem, positions, g_mix, w_in, b_gate, sink, w_branch_a, w_branch_b, w_out,
           g_cross, g_mem, w_cq, w_ckv, w_co, g_mlp, w_1, w_2, g_final):
    bf = jnp.bfloat16
    B, S, _ = x.shape
    depth = g_mix.shape[0]
    inv = ROPE_THETA ** (-jnp.arange(HALF, dtype=jnp.float32) / HALF)
    inv4 = jnp.tile(inv, HG)[None]
    pos = jnp.repeat(positions.astype(jnp.float32).reshape(B, S // HG, HG), HALF, axis=-1)
    for l in range(depth):
        nat, *res = _proj(x, g_mix[l][None], _permute_w_in(w_in[l]),
                          b_gate[l].reshape(1, GATE_WIDTH), pos, inv4)
        o_dil, lse_dil = [], []
        for gi, (_, d) in enumerate(DIL_GROUPS):
            if gi == 0:
                o, lse = _attn(nat, COL_D1 // GROUP_W, sink[l], has_sink=False, write_lse=True)
            else:
                o, lse = _attn(res[gi - 1].reshape(B * d, S // d, QKV_W), 0, sink[l],
                               has_sink=False, write_lse=True)
            o_dil.append(o)
            lse_dil.append(lse)
        o_swa = _attn(nat, COL_SWA // GROUP_W, sink[l], has_sink=True, write_lse=False)
        x = _merge(x, nat, o_dil, lse_dil, o_swa, w_branch_a[l].astype(bf),
                   w_branch_b[l].astype(bf), w_out[l].astype(bf))
        kv = _memkv(mem, g_mem[l][None], w_ckv[l].astype(bf))
        x = _cross(x, g_cross[l][None], (w_cq[l] * (X_HEAD_DIM ** -0.5)).astype(bf), kv,
                   w_co[l].astype(bf))
        x = _mlp(x, g_mlp[l][None], w_1[l].astype(bf), w_2[l].astype(bf), g_final[None],
                 final_norm=(l == depth - 1))
    return x
```

```python
import functools

import numpy as np
import jax
import jax.numpy as jnp
from jax import lax
from jax.experimental import pallas as pl
from jax.experimental.pallas import tpu as pltpu

D_MODEL = 1024
HEAD_DIM = 64
HALF = HEAD_DIM // 2
DIL_GROUPS = ((128, 1), (512, 4), (2048, 16))
N_DIL = len(DIL_GROUPS)
DIL_HEADS = 8
SWA_Q_HEADS = 8
SWA_KV_HEADS = 2
BLOCK = 128
ROPE_THETA = 10000.0
X_HEADS = 4
X_HEAD_DIM = D_MODEL // X_HEADS
D_FF = 4 * D_MODEL
EPS = 1e-6
N_BRANCH = 2

DIL_WIDTH = N_DIL * DIL_HEADS * HEAD_DIM
DIL_OUT = DIL_HEADS * HEAD_DIM
SWA_Q_WIDTH = SWA_Q_HEADS * HEAD_DIM
SWA_KV_WIDTH = SWA_KV_HEADS * HEAD_DIM
GATE_WIDTH = N_BRANCH * D_MODEL

LANES = 128
HG = 4
HGW = HG * HEAD_DIM
GROUP_W = 2 * HGW
QKV_W = 3 * GROUP_W

COL_GATE = 0
COL_D1 = GATE_WIDTH
COL_SWA = COL_D1 + QKV_W
NAT_W = COL_SWA + QKV_W
PROJ_W = NAT_W + (N_DIL - 1) * QKV_W

TM = 512
TP = 1024
SUB = 512
QT = 4096
NEG = -1e30
VMEM_LIMIT = 56 * 1024 * 1024


def _const_spec(shape):
    nd = len(shape)
    return pl.BlockSpec(shape, lambda *_: (0,) * nd, pipeline_mode=pl.Buffered(1))


def _params(n_axes):
    return pltpu.CompilerParams(dimension_semantics=("parallel",) * n_axes,
                                vmem_limit_bytes=VMEM_LIMIT)


def _rms(x, g):
    ms = jnp.mean(x * x, axis=-1, keepdims=True)
    return x * lax.rsqrt(ms + EPS) * g


def _rope(r, cos, sin):
    a = r[:, :LANES]
    b = r[:, LANES:]
    return jnp.concatenate([a * cos - b * sin, b * cos + a * sin], axis=1)


def _proj_kernel(x_ref, g_ref, w_ref, bg_ref, pos_ref, inv_ref, nat_ref, *rest):
    res_refs = rest[:N_DIL - 1]
    hs_ref, hp_ref, cos_ref, sin_ref = rest[N_DIL - 1:]
    h = _rms(x_ref[...], g_ref[...])
    hn = h.astype(jnp.bfloat16)
    ang = pos_ref[...] * inv_ref[...]
    lane_grp = lax.broadcasted_iota(jnp.int32, (1, LANES), 1) // HALF
    for trig, t_ref in ((jnp.cos, cos_ref), (jnp.sin, sin_ref)):
        t = trig(ang)
        rolled = [t] + [pltpu.roll(t, HALF * k, 1) for k in range(1, HG)]
        for q in range(HG):
            rep = rolled[(0 - q) % HG]
            for k in range(1, HG):
                rep = jnp.where(lane_grp == k, rolled[(k - q) % HG], rep)
            t_ref[pl.ds(q, TM // HG, stride=HG), :] = rep
    cos = cos_ref[...]
    sin = sin_ref[...]

    for j in range(D_MODEL // LANES):
        hs_ref[j] = h[:, j * LANES:(j + 1) * LANES]
    for gi in range(N_DIL - 1):
        d = DIL_GROUPS[gi + 1][1]
        tmd = TM // d
        for r in range(d):
            for j in range(D_MODEL // LANES):
                hp_ref[gi, r * tmd:(r + 1) * tmd, j * LANES:(j + 1) * LANES] = (
                    hs_ref[j, pl.ds(r, tmd, stride=d), :].astype(jnp.bfloat16))

    for c in range(0, NAT_W, HGW):
        r = jnp.dot(hn, w_ref[:, c:c + HGW], preferred_element_type=jnp.float32)
        if c < COL_D1:
            z = r + bg_ref[:, c:c + HGW]
            out = 0.5 * jnp.tanh(0.5 * z) + 0.5
        elif (c - COL_D1) % QKV_W < 2 * GROUP_W:
            out = _rope(r, cos, sin)
        else:
            out = r
        nat_ref[:, c:c + HGW] = out.astype(jnp.bfloat16)

    for gi, o_ref in enumerate(res_refs):
        d = DIL_GROUPS[gi + 1][1]
        tmd = TM // d
        cosp = jnp.concatenate([cos_ref[pl.ds(r, tmd, stride=d), :] for r in range(d)], axis=0)
        sinp = jnp.concatenate([sin_ref[pl.ds(r, tmd, stride=d), :] for r in range(d)], axis=0)
        hp = hp_ref[gi]
        col0 = NAT_W + gi * QKV_W
        for c in range(0, QKV_W, HGW):
            rr = jnp.dot(hp, w_ref[:, col0 + c:col0 + c + HGW],
                         preferred_element_type=jnp.float32)
            out = (_rope(rr, cosp, sinp) if c < 2 * GROUP_W else rr).astype(jnp.bfloat16)
            for r in range(d):
                o_ref[r, :, c:c + HGW] = out[r * tmd:(r + 1) * tmd]


def _proj(x, g, w, bg, pos, inv4):
    B, S, _ = x.shape
    tok = lambda w_: pl.BlockSpec((None, TM, w_), lambda b, i: (b, i, 0))
    out_specs = [tok(NAT_W)]
    out_shape = [jax.ShapeDtypeStruct((B, S, NAT_W), jnp.bfloat16)]
    for _, d in DIL_GROUPS[1:]:
        out_specs.append(pl.BlockSpec((None, d, TM // d, QKV_W), lambda b, i: (b, 0, i, 0)))
        out_shape.append(jax.ShapeDtypeStruct((B, d, S // d, QKV_W), jnp.bfloat16))
    return pl.pallas_call(
        _proj_kernel,
        name="proj",
        grid=(B, S // TM),
        in_specs=[tok(D_MODEL), _const_spec((1, D_MODEL)), _const_spec((D_MODEL, PROJ_W)),
                  _const_spec((1, GATE_WIDTH)),
                  pl.BlockSpec((None, TM // HG, LANES), lambda b, i: (b, i, 0)),
                  _const_spec((1, LANES))],
        out_specs=out_specs,
        out_shape=out_shape,
        scratch_shapes=[pltpu.VMEM((D_MODEL // LANES, TM, LANES), jnp.float32),
                        pltpu.VMEM((N_DIL - 1, TM, D_MODEL), jnp.bfloat16),
                        pltpu.VMEM((TM, LANES), jnp.float32),
                        pltpu.VMEM((TM, LANES), jnp.float32)],
        compiler_params=_params(2),
    )(x, g, w, bg, pos, inv4)


def _attn_kernel(sink_ref, q_ref, kc_ref, kp_ref, vc_ref, vp_ref, o_ref, *rest,
                 has_sink, write_lse):
    lse_ref = rest[0] if write_lse else None
    first = pl.program_id(1) == 0
    nseq, qt, _ = q_ref.shape

    row = lax.broadcasted_iota(jnp.int32, (BLOCK, 2 * BLOCK), 0)
    col = lax.broadcasted_iota(jnp.int32, (BLOCK, 2 * BLOCK), 1)
    band = (col >= row) & (col <= row + BLOCK)
    bias = jnp.where(band, 0.0, NEG).astype(jnp.float32)
    bias_first = jnp.where(band & ((col >= BLOCK) | jnp.logical_not(first)), 0.0, NEG
                           ).astype(jnp.float32)

    lane = lax.broadcasted_iota(jnp.int32, (1, HGW), 1)
    qmask = [jnp.where((lane % LANES) // HALF == j, 1.0, 0.0).astype(jnp.bfloat16)
             for j in range(HG)]
    vhead = lane // HEAD_DIM
    lse_lane = lax.broadcasted_iota(jnp.int32, (BLOCK, LANES), 1)

    for n in range(nseq):
        for qb in range(qt // BLOCK):
            rows = slice(qb * BLOCK, (qb + 1) * BLOCK)
            lse_tile = jnp.zeros((BLOCK, LANES), jnp.float32)
            for g in range(2):
                cs = slice(g * HGW, (g + 1) * HGW)
                if qb == 0:
                    kcat = jnp.concatenate([kp_ref[n, :, cs], kc_ref[n, rows, cs]], axis=0)
                    vcat = jnp.concatenate([vp_ref[n, :, cs], vc_ref[n, rows, cs]], axis=0)
                else:
                    kcat = kc_ref[n, (qb - 1) * BLOCK:(qb + 1) * BLOCK, cs]
                    vcat = vc_ref[n, (qb - 1) * BLOCK:(qb + 1) * BLOCK, cs]
                bias_q = bias_first if qb == 0 else bias
                q4 = q_ref[n, rows, cs]
                qs = jnp.concatenate([q4 * qmask[j] for j in range(HG)], axis=0)
                s_all = lax.dot_general(qs, kcat, (((1,), (1,)), ((), ())),
                                        preferred_element_type=jnp.float32)
                ps = []
                for j in range(HG):
                    s = s_all[j * BLOCK:(j + 1) * BLOCK] + bias_q
                    m = jnp.max(s, axis=-1, keepdims=True)
                    if has_sink:
                        sk = sink_ref[g * HG + j]
                        m = jnp.maximum(m, sk)
                    p = jnp.exp(s - m)
                    den = jnp.sum(p, axis=-1, keepdims=True)
                    if has_sink:
                        den = den + jnp.exp(sk - m)
                    ps.append((p * (1.0 / den)).astype(jnp.bfloat16))
                    if write_lse:
                        lse = m + jnp.log(den)
                        lse_tile = jnp.where(lse_lane == g * HG + j, lse, lse_tile)
                o_s = jnp.dot(jnp.concatenate(ps, axis=0), vcat,
                              preferred_element_type=jnp.float32)
                o4 = o_s[:BLOCK]
                for j in range(1, HG):
                    o4 = jnp.where(vhead == j, o_s[j * BLOCK:(j + 1) * BLOCK], o4)
                o_ref[n, rows, cs] = o4.astype(jnp.bfloat16)
            if write_lse:
                lse_ref[n, rows, :] = lse_tile


def _attn(qkv, cb0, sink, *, has_sink, write_lse):
    N, L, _ = qkv.shape
    qt = min(QT, L)
    ns = QT // qt
    r = qt // BLOCK
    cur = lambda cb: pl.BlockSpec((ns, qt, GROUP_W), lambda n, i: (n, i, cb))
    prev = lambda cb: pl.BlockSpec((ns, BLOCK, GROUP_W),
                                   lambda n, i: (n, jnp.maximum(i * r - 1, 0), cb))
    out_specs = [pl.BlockSpec((ns, qt, GROUP_W), lambda n, i: (n, i, 0))]
    out_shape = [jax.ShapeDtypeStruct((N, L, GROUP_W), jnp.bfloat16)]
    if write_lse:
        out_specs.append(pl.BlockSpec((ns, qt, LANES), lambda n, i: (n, i, 0)))
        out_shape.append(jax.ShapeDtypeStruct((N, L, LANES), jnp.float32))
    res = pl.pallas_call(
        functools.partial(_attn_kernel, has_sink=has_sink, write_lse=write_lse),
        name="attn_sink" if has_sink else "attn",
        grid=(N // ns, L // qt),
        in_specs=[pl.BlockSpec(memory_space=pltpu.SMEM),
                  cur(cb0), cur(cb0 + 1), prev(cb0 + 1), cur(cb0 + 2), prev(cb0 + 2)],
        out_specs=out_specs,
        out_shape=out_shape,
        compiler_params=_params(2),
    )(sink, qkv, qkv, qkv, qkv, qkv)
    return res if write_lse else res[0]


def _expand_matrix():
    e = np.zeros((2 * LANES, N_DIL * DIL_OUT), np.float32)
    for g in range(N_DIL):
        for hh in range(DIL_HEADS):
            c0 = g * DIL_OUT + hh * HEAD_DIM
            e[g * DIL_HEADS + hh, c0:c0 + HEAD_DIM] = 1.0
            e[LANES + g * DIL_HEADS + hh, c0:c0 + HEAD_DIM] = 1.0
    return jnp.asarray(e, jnp.bfloat16)


def _merge_kernel(x_ref, o0_ref, o1_ref, o2_ref, l0_ref, l1_ref, l2_ref, ob_ref, gt_ref,
                  e_ref, wa_ref, wb_ref, wo_ref, out_ref, onat_ref, lnat_ref):
    res = ((DIL_GROUPS[1][1], o1_ref, l1_ref), (DIL_GROUPS[2][1], o2_ref, l2_ref))
    nslab = DIL_OUT // LANES
    for gi, (d, o_ref, l_ref) in enumerate(res):
        for r in range(d):
            lnat_ref[gi, pl.ds(r, TP // d, stride=d), :] = l_ref[r]
            of = o_ref[r].astype(jnp.float32)
            for s in range(nslab):
                onat_ref[gi * nslab + s, pl.ds(r, TP // d, stride=d), :] = (
                    of[:, s * LANES:(s + 1) * LANES])
    lane = lax.broadcasted_iota(jnp.int32, (1, LANES), 1)
    for r0 in range(0, TP, SUB):
        rows = slice(r0, r0 + SUB)
        ls = [l0_ref[rows, :], lnat_ref[0, rows, :], lnat_ref[1, rows, :]]
        mx = jnp.maximum(jnp.maximum(ls[0], ls[1]), ls[2])
        es = [jnp.exp(l - mx) for l in ls]
        inv = 1.0 / (es[0] + es[1] + es[2])
        a = jnp.zeros((SUB, LANES), jnp.float32)
        for gi, e in enumerate(es):
            ag = jnp.where(lane < DIL_HEADS, e * inv, 0.0)
            a = a + (pltpu.roll(ag, gi * DIL_HEADS, 1) if gi else ag)
        a_hi = a.astype(jnp.bfloat16)
        a_lo = (a - a_hi.astype(jnp.float32)).astype(jnp.bfloat16)
        aa = jnp.concatenate([a_hi, a_lo], axis=1)
        oa = jnp.dot(aa, e_ref[:, :DIL_OUT], preferred_element_type=jnp.float32
                     ) * o0_ref[rows, :].astype(jnp.float32)
        for gi in range(len(res)):
            og = jnp.concatenate([onat_ref[gi * nslab + s, rows, :] for s in range(nslab)],
                                 axis=1)
            wide = jnp.dot(aa, e_ref[:, (gi + 1) * DIL_OUT:(gi + 2) * DIL_OUT],
                           preferred_element_type=jnp.float32)
            oa = oa + wide * og
        ya = jnp.dot(oa.astype(jnp.bfloat16), wa_ref[...], preferred_element_type=jnp.float32)
        yb = jnp.dot(ob_ref[rows, :], wb_ref[...], preferred_element_type=jnp.float32)
        merged = (gt_ref[rows, :D_MODEL].astype(jnp.float32) * ya
                  + gt_ref[rows, D_MODEL:].astype(jnp.float32) * yb)
        out_ref[rows, :] = x_ref[rows, :] + jnp.dot(merged.astype(jnp.bfloat16), wo_ref[...],
                                                    preferred_element_type=jnp.float32)


def _merge(x, nat, o_dil, lse_dil, o_swa, wa, wb, wo):
    B, S, _ = x.shape
    tok = lambda w, cb=0: pl.BlockSpec((None, TP, w), lambda b, i: (b, i, cb))
    res = lambda d, w: pl.BlockSpec((None, d, TP // d, w), lambda b, i: (b, 0, i, 0))
    d1, d2 = DIL_GROUPS[1][1], DIL_GROUPS[2][1]
    return pl.pallas_call(
        _merge_kernel,
        name="merge",
        grid=(B, S // TP),
        in_specs=[tok(D_MODEL), tok(GROUP_W), res(d1, GROUP_W), res(d2, GROUP_W),
                  tok(LANES), res(d1, LANES), res(d2, LANES),
                  tok(GROUP_W), tok(GATE_WIDTH, COL_GATE // GATE_WIDTH),
                  _const_spec((2 * LANES, N_DIL * DIL_OUT)),
                  _const_spec((DIL_OUT, D_MODEL)), _const_spec((SWA_Q_WIDTH, D_MODEL)),
                  _const_spec((D_MODEL, D_MODEL))],
        out_specs=tok(D_MODEL),
        out_shape=jax.ShapeDtypeStruct((B, S, D_MODEL), jnp.float32),
        scratch_shapes=[pltpu.VMEM(((N_DIL - 1) * DIL_OUT // LANES, TP, LANES), jnp.float32),
                        pltpu.VMEM((N_DIL - 1, TP, LANES), jnp.float32)],
        compiler_params=_params(2),
    )(x, o_dil[0], o_dil[1].reshape(B, d1, S // d1, GROUP_W),
      o_dil[2].reshape(B, d2, S // d2, GROUP_W),
      lse_dil[0], lse_dil[1].reshape(B, d1, S // d1, LANES),
      lse_dil[2].reshape(B, d2, S // d2, LANES),
      o_swa, nat, _expand_matrix(), wa, wb, wo)


def _memkv_kernel(mem_ref, g_ref, w_ref, kv_ref):
    mn = _rms(mem_ref[...], g_ref[...]).astype(jnp.bfloat16)
    kv_ref[...] = jnp.dot(mn, w_ref[...], preferred_element_type=jnp.float32
                          ).astype(jnp.bfloat16)


def _memkv(mem, g, w):
    B, M, _ = mem.shape
    return pl.pallas_call(
        _memkv_kernel,
        name="memkv",
        grid=(B,),
        in_specs=[pl.BlockSpec((None, M, D_MODEL), lambda b: (b, 0, 0)),
                  _const_spec((1, D_MODEL)), _const_spec((D_MODEL, 2 * D_MODEL))],
        out_specs=pl.BlockSpec((None, M, 2 * D_MODEL), lambda b: (b, 0, 0)),
        out_shape=jax.ShapeDtypeStruct((B, M, 2 * D_MODEL), jnp.bfloat16),
        compiler_params=_params(1),
    )(mem, g, w)


def _cross_kernel(x_ref, g_ref, wq_ref, kv_ref, wo_ref, out_ref):
    for r0 in range(0, TP, SUB):
        rows = slice(r0, r0 + SUB)
        x = x_ref[rows, :]
        hc = _rms(x, g_ref[...]).astype(jnp.bfloat16)
        q = jnp.dot(hc, wq_ref[...], preferred_element_type=jnp.float32).astype(jnp.bfloat16)
        os = []
        for hh in range(X_HEADS):
            cs = slice(hh * X_HEAD_DIM, (hh + 1) * X_HEAD_DIM)
            s = lax.dot_general(q[:, cs], kv_ref[:, cs], (((1,), (1,)), ((), ())),
                                preferred_element_type=jnp.float32)
            m = jnp.max(s, axis=-1, keepdims=True)
            p = jnp.exp(s - m)
            p = p * (1.0 / jnp.sum(p, axis=-1, keepdims=True))
            vs = slice(D_MODEL + hh * X_HEAD_DIM, D_MODEL + (hh + 1) * X_HEAD_DIM)
            os.append(jnp.dot(p.astype(jnp.bfloat16), kv_ref[:, vs],
                              preferred_element_type=jnp.float32).astype(jnp.bfloat16))
        o = jnp.concatenate(os, axis=1)
        out_ref[rows, :] = x + jnp.dot(o, wo_ref[...], preferred_element_type=jnp.float32)


def _cross(x, g, wq, kv, wo):
    B, S, _ = x.shape
    M = kv.shape[1]
    tok = pl.BlockSpec((None, TP, D_MODEL), lambda b, i: (b, i, 0))
    return pl.pallas_call(
        _cross_kernel,
        name="cross",
        grid=(B, S // TP),
        in_specs=[tok, _const_spec((1, D_MODEL)), _const_spec((D_MODEL, D_MODEL)),
                  pl.BlockSpec((None, M, 2 * D_MODEL), lambda b, i: (b, 0, 0)),
                  _const_spec((D_MODEL, D_MODEL))],
        out_specs=tok,
        out_shape=jax.ShapeDtypeStruct((B, S, D_MODEL), jnp.float32),
        compiler_params=_params(2),
    )(x, g, wq, kv, wo)


FF_CHUNK = 1024


def _mlp_kernel(x_ref, g_ref, w1_ref, w2_ref, gf_ref, out_ref, *, final_norm):
    for r0 in range(0, TP, SUB):
        rows = slice(r0, r0 + SUB)
        x = x_ref[rows, :]
        h = _rms(x, g_ref[...]).astype(jnp.bfloat16)
        y = x
        for c in range(0, D_FF, FF_CHUNK):
            a = jnp.maximum(jnp.dot(h, w1_ref[:, c:c + FF_CHUNK],
                                    preferred_element_type=jnp.float32), 0.0)
            y = y + jnp.dot((a * a).astype(jnp.bfloat16), w2_ref[c:c + FF_CHUNK, :],
                            preferred_element_type=jnp.float32)
        out_ref[rows, :] = _rms(y, gf_ref[...]) if final_norm else y


def _mlp(x, g, w1, w2, gf, final_norm):
    B, S, _ = x.shape
    tok = pl.BlockSpec((None, TP, D_MODEL), lambda b, i: (b, i, 0))
    return pl.pallas_call(
        functools.partial(_mlp_kernel, final_norm=final_norm),
        name="mlp",
        grid=(B, S // TP),
        in_specs=[tok, _const_spec((1, D_MODEL)), _const_spec((D_MODEL, D_FF)),
                  _const_spec((D_FF, D_MODEL)), _const_spec((1, D_MODEL))],
        out_specs=tok,
        out_shape=jax.ShapeDtypeStruct((B, S, D_MODEL), jnp.float32),
        compiler_params=_params(2),
    )(x, g, w1, w2, gf)


def _split_half(w, n_heads):
    k = w.shape[0]
    w = w.reshape(k, n_heads // HG, HG, 2, HALF)
    return w.transpose(0, 1, 3, 2, 4).reshape(k, n_heads * HEAD_DIM)


def _permute_w_in(w_in):
    scale = HEAD_DIM ** -0.5
    c = 0
    qa = w_in[:, c:c + DIL_WIDTH]; c += DIL_WIDTH
    ka = w_in[:, c:c + DIL_WIDTH]; c += DIL_WIDTH
    va = w_in[:, c:c + DIL_WIDTH]; c += DIL_WIDTH
    qb = w_in[:, c:c + SWA_Q_WIDTH]; c += SWA_Q_WIDTH
    kb = w_in[:, c:c + SWA_KV_WIDTH]; c += SWA_KV_WIDTH
    vb = w_in[:, c:c + SWA_KV_WIDTH]; c += SWA_KV_WIDTH
    gt = w_in[:, c:]
    kd = w_in.shape[0]

    def dil(g):
        sl = slice(g * GROUP_W, (g + 1) * GROUP_W)
        return [_split_half(qa[:, sl] * scale, DIL_HEADS), _split_half(ka[:, sl], DIL_HEADS),
                va[:, sl]]

    kb = kb.reshape(kd, SWA_KV_HEADS, 2, 1, HALF)
    kb = jnp.broadcast_to(kb, (kd, SWA_KV_HEADS, 2, HG, HALF)).reshape(kd, GROUP_W)
    vb = vb.reshape(kd, SWA_KV_HEADS, 1, HEAD_DIM)
    vb = jnp.broadcast_to(vb, (kd, SWA_KV_HEADS, HG, HEAD_DIM)).reshape(kd, GROUP_W)
    cols = [gt] + dil(0) + [_split_half(qb * scale, SWA_Q_HEADS), kb, vb]
    for g in range(1, N_DIL):
        cols += dil(g)
    return jnp.concatenate(cols, axis=1).astype(jnp.bfloat16)


def kernel(x, mem, positions, g_mix, w_in, b_gate, sink, w_branch_a, w_branch_b, w_out,
           g_cross, g_mem, w_cq, w_ckv, w_co, g_mlp, w_1, w_2, g_final):
    bf = jnp.bfloat16
    B, S, _ = x.shape
    depth = g_mix.shape[0]
    inv = ROPE_THETA ** (-jnp.arange(HALF, dtype=jnp.float32) / HALF)
    inv4 = jnp.tile(inv, HG)[None]
    pos = jnp.repeat(positions.astype(jnp.float32).reshape(B, S // HG, HG), HALF, axis=-1)
    for l in range(depth):
        nat, *res = _proj(x, g_mix[l][None], _permute_w_in(w_in[l]),
                          b_gate[l].reshape(1, GATE_WIDTH), pos, inv4)
        o_dil, lse_dil = [], []
        for gi, (_, d) in enumerate(DIL_GROUPS):
            if gi == 0:
                o, lse = _attn(nat, COL_D1 // GROUP_W, sink[l], has_sink=False, write_lse=True)
            else:
                o, lse = _attn(res[gi - 1].reshape(B * d, S // d, QKV_W), 0, sink[l],
                               has_sink=False, write_lse=True)
            o_dil.append(o)
            lse_dil.append(lse)
        o_swa = _attn(nat, COL_SWA // GROUP_W, sink[l], has_sink=True, write_lse=False)
        x = _merge(x, nat, o_dil, lse_dil, o_swa, w_branch_a[l].astype(bf),
                   w_branch_b[l].astype(bf), w_out[l].astype(bf))
        kv = _memkv(mem, g_mem[l][None], w_ckv[l].astype(bf))
        x = _cross(x, g_cross[l][None], (w_cq[l] * (X_HEAD_DIM ** -0.5)).astype(bf), kv,
                   w_co[l].astype(bf))
        x = _mlp(x, g_mlp[l][None], w_1[l].astype(bf), w_2[l].astype(bf), g_final[None],
                 final_norm=(l == depth - 1))
    return x
```
